```python
import functools
import jax
import jax.numpy as jnp
from jax import lax
import numpy as np

D_MODEL = 1024
BATCH = 4
SEQ = 4096
DEPTH = 2
DEC_BATCH = 128
DEC_SEQ = 4
PAST_LEN = 2048
PAGE_SIZE = 128

DH_A = 128
H_A = D_MODEL // DH_A
SB_BLOCK = 128
SB_BIAS_NEAR = -2.0
SB_BIAS_FAR = -10.0
H_B = 4
DK_B = D_MODEL // (2 * H_B)
DV_B = D_MODEL // H_B
GATE_RANK = 16
GATE_NORM = 16.0
GLA_CHUNK = 64
N_GROUPS = 4
EXPERTS_PER_GROUP = 8
N_EXPERTS = N_GROUPS * EXPERTS_PER_GROUP
TOP_K = 2
D_EXPERT = D_MODEL // 4
MOE_BLOCK = 128
EPS = 1e-6
IN_SPLITS = (H_A * DH_A, H_A * DH_A, H_A * DH_A, H_B * DK_B, H_B * DK_B, H_B * DV_B, GATE_RANK, D_MODEL, D_MODEL)
N_IN = sum(IN_SPLITS)

kernel_name = "hybrid_sb_gla_hmoe_step"


def rmsnorm(x, g):
    xf = x.astype(jnp.float32)
    y = xf * lax.rsqrt(jnp.mean(xf * xf, axis=-1, keepdims=True) + EPS)
    return (y * g.astype(jnp.float32)).astype(x.dtype)


def stick_breaking(q, k, v, bias, q_pos, k_pos):
    z = jnp.einsum("bqhd,bkhd->bhqk", q.astype(jnp.float32), k.astype(jnp.float32)) * (DH_A ** -0.5)
    z = z + bias.astype(jnp.float32)[None, :, None, None]
    before = k_pos[None, :] < q_pos[:, None]
    log_keep = jnp.where(before, jax.nn.log_sigmoid(-z), 0.0)
    between = lax.cumsum(log_keep, axis=3, reverse=True) - log_keep
    w = jnp.where(before, jnp.exp(jax.nn.log_sigmoid(z) + between), 0.0)
    return jnp.einsum("bhqk,bkhd->bqhd", w, v.astype(jnp.float32)).astype(v.dtype)


def sb_prompt(q, k, v, bias):
    b, t, h, dh = q.shape
    nb = t // SB_BLOCK
    pos = jnp.arange(t, dtype=jnp.int32)
    q_blocks = q.reshape(b, nb, SB_BLOCK, h, dh).transpose(1, 0, 2, 3, 4)
    out = lax.map(lambda qp: stick_breaking(qp[0], k, v, bias, qp[1], pos), (q_blocks, pos.reshape(nb, SB_BLOCK)))
    return out.transpose(1, 0, 2, 3, 4).reshape(b, t, h, dh)


def sb_sample(q, k, v, bias, past_k, past_v):
    p, t = past_k.shape[1], q.shape[1]
    k_all = jnp.concatenate([past_k.astype(k.dtype), k], axis=1)
    v_all = jnp.concatenate([past_v.astype(v.dtype), v], axis=1)
    q_pos = p + jnp.arange(t, dtype=jnp.int32)
    k_pos = jnp.arange(p + t, dtype=jnp.int32)
    return stick_breaking(q, k_all, v_all, bias, q_pos, k_pos)


def gla_chunked(q, k, v, log_a, s0, chunk):
    b, t, h, dk = q.shape
    n = t // chunk

    def to_chunks(a):
        return a.reshape(b, n, chunk, h, a.shape[-1]).transpose(1, 0, 2, 3, 4).astype(jnp.float32)

    tri = jnp.tril(jnp.ones((chunk, chunk), dtype=bool))

    def step(s, inp):
        qc, kc, vc, gc = inp
        cum = jnp.cumsum(gc, axis=1)
        last = cum[:, -1:]
        q_dec = qc * jnp.exp(cum)
        k_dec = kc * jnp.exp(-cum)
        att = jnp.where(tri, jnp.einsum("bthk,bshk->bhts", q_dec, k_dec), 0.0)
        o = jnp.einsum("bthk,bhkv->bthv", q_dec, s) + jnp.einsum("bhts,bshv->bthv", att, vc)
        k_rem = kc * jnp.exp(last - cum)
        s = jnp.exp(last[:, 0])[..., None] * s + jnp.einsum("bshk,bshv->bhkv", k_rem, vc)
        return s, o

    s_fin, o = lax.scan(step, s0.astype(jnp.float32), (to_chunks(q), to_chunks(k), to_chunks(v), to_chunks(log_a)))
    o = o.transpose(1, 0, 2, 3, 4).reshape(b, t, h, v.shape[-1])
    return o, s_fin


def moe_dispatch(x, expert_id, weight, w_gate, w_up, w_down):
    n, d = x.shape
    m = n * TOP_K
    n_blocks = (m + N_EXPERTS * (MOE_BLOCK - 1) + MOE_BLOCK - 1) // MOE_BLOCK
    cap = n_blocks * MOE_BLOCK
    e_flat = expert_id.reshape(m)
    order = jnp.argsort(e_flat)
    e_sorted = e_flat[order]
    tok_sorted = (order // TOP_K).astype(jnp.int32)
    w_sorted = weight.reshape(m)[order]
    counts = jnp.zeros((N_EXPERTS,), jnp.int32).at[e_flat].add(1)
    padded = (counts + MOE_BLOCK - 1) // MOE_BLOCK * MOE_BLOCK
    start = jnp.cumsum(counts) - counts
    pend = jnp.cumsum(padded)
    pstart = pend - padded
    dest = pstart[e_sorted] + jnp.arange(m, dtype=jnp.int32) - start[e_sorted]
    slot_tok = jnp.zeros((cap,), jnp.int32).at[dest].set(tok_sorted)
    slot_w = jnp.zeros((cap,), jnp.float32).at[dest].set(w_sorted)
    block_start = jnp.arange(n_blocks, dtype=jnp.int32) * MOE_BLOCK
    block_exp = jnp.minimum(jnp.searchsorted(pend, block_start, side="right"), N_EXPERTS - 1)
    xb = x[slot_tok].reshape(n_blocks, MOE_BLOCK, d)

    def expert_block(args):
        xe, e = args
        hid = jax.nn.silu(xe @ w_gate[e]) * (xe @ w_up[e])
        return hid @ w_down[e]

    yb = lax.map(expert_block, (xb, block_exp)).reshape(cap, d)
    return jnp.zeros_like(x).at[slot_tok].add(yb * slot_w[:, None].astype(x.dtype))


def hier_moe(h, lp):
    shp = h.shape
    x = h.reshape(-1, shp[-1])
    n = x.shape[0]
    xf = x.astype(jnp.float32)
    g_logits = xf @ lp["w_rg"].astype(jnp.float32) + lp["b_rg"].astype(jnp.float32)
    g_idx = jnp.argmax(g_logits, axis=-1)
    g_w = jnp.take_along_axis(jax.nn.softmax(g_logits, axis=-1), g_idx[:, None], axis=1)
    e_logits = (xf @ lp["w_re"].astype(jnp.float32) + lp["b_re"].astype(jnp.float32)).reshape(n, N_GROUPS, EXPERTS_PER_GROUP)
    e_logits = jnp.take_along_axis(e_logits, g_idx[:, None, None], axis=1)[:, 0]
    top_v, top_i = lax.top_k(e_logits, TOP_K)
    weight = jax.nn.softmax(top_v, axis=-1) * g_w
    expert_id = (g_idx[:, None] * EXPERTS_PER_GROUP + top_i).astype(jnp.int32)
    y = moe_dispatch(x, expert_id, weight, lp["w_e_gate"], lp["w_e_up"], lp["w_e_down"])
    return y.reshape(shp)


def token_mixer(h, lp, sb_fn, gla_s0, gla_chunk):
    b, t, _ = h.shape
    proj = h @ lp["w_in"]
    offs = [int(o) for o in np.cumsum(IN_SPLITS)[:-1]]
    qa, ka, va, qb, kb, vb, a_low, g_a, g_b = jnp.split(proj, offs, axis=-1)
    qa = rmsnorm(qa.reshape(b, t, H_A, DH_A), lp["qn_g"])
    ka = rmsnorm(ka.reshape(b, t, H_A, DH_A), lp["kn_g"])
    va = va.reshape(b, t, H_A, DH_A)
    o_a = sb_fn(qa, ka, va, lp["sb_bias"])
    qb = qb.reshape(b, t, H_B, DK_B) * (DK_B ** -0.5)
    kb = kb.reshape(b, t, H_B, DK_B)
    vb = vb.reshape(b, t, H_B, DV_B)
    log_a = jax.nn.log_sigmoid((a_low @ lp["w_gk2"] + lp["b_gk2"]).astype(jnp.float32)) / GATE_NORM
    o_b, s_new = gla_chunked(qb, kb, vb, log_a.reshape(b, t, H_B, DK_B), gla_s0, gla_chunk)
    o_b = rmsnorm(o_b, lp["onorm_g"]).astype(h.dtype)
    merged = jax.nn.sigmoid(g_a) * o_a.reshape(b, t, D_MODEL) + jax.nn.sigmoid(g_b) * o_b.reshape(b, t, D_MODEL)
    return merged @ lp["w_out"], ka, va, s_new


def trunk_layer(x, c, lp, sb_fn, gla_s0, gla_chunk):
    mod = jax.nn.silu(c) @ lp["w_ada"] + lp["b_ada"]
    sh1, sc1, gt1, sh2, sc2, gt2 = jnp.split(mod[:, None, :], 6, axis=-1)
    h = rmsnorm(x, lp["norm1_g"]) * (1.0 + sc1) + sh1
    mix, k_rows, v_rows, s_new = token_mixer(h, lp, sb_fn, gla_s0, gla_chunk)
    x = x + gt1 * mix
    h = rmsnorm(x, lp["norm2_g"]) * (1.0 + sc2) + sh2
    x = x + gt2 * hier_moe(h, lp)
    return x, k_rows, v_rows, s_new


def setup_inputs(seed: int = 0) -> dict:
    key = jax.random.key(seed)
    ks = jax.random.split(key, 32)
    f32 = jnp.float32
    n_pages = PAST_LEN // PAGE_SIZE
    used = DEC_BATCH * n_pages
    n_pool = used + max(1, used // 4)

    def nrm(k, shape, scale=1.0):
        return jax.random.normal(k, shape, f32) * scale

    def gain(k, shape):
        return 1.0 + 0.02 * jax.random.normal(k, shape, f32)

    page_table = jax.random.permutation(ks[5], n_pool)[:used].reshape(DEC_BATCH, n_pages).astype(jnp.int32)
    sb_bias = jnp.linspace(SB_BIAS_NEAR, SB_BIAS_FAR, H_A, dtype=f32)[None, :] + nrm(ks[26], (DEPTH, H_A), 0.1)
    return {
        "x_prompt": nrm(ks[0], (BATCH, SEQ, D_MODEL)),
        "x_sample": nrm(ks[1], (DEC_BATCH, DEC_SEQ, D_MODEL)),
        "cache_k": nrm(ks[2], (DEPTH, n_pool, PAGE_SIZE, H_A, DH_A)),
        "cache_v": nrm(ks[3], (DEPTH, n_pool, PAGE_SIZE, H_A, DH_A)),
        "state_gla": nrm(ks[4], (DEPTH, DEC_BATCH, H_B, DK_B, DV_B)),
        "page_table": page_table,
        "c_prompt": nrm(ks[6], (BATCH, D_MODEL)),
        "c_sample": nrm(ks[7], (DEC_BATCH, D_MODEL)),
        "norm1_g": gain(ks[8], (DEPTH, D_MODEL)),
        "norm2_g": gain(ks[9], (DEPTH, D_MODEL)),
        "w_ada": nrm(ks[10], (DEPTH, D_MODEL, 6 * D_MODEL), 0.5 * D_MODEL ** -0.5),
        "b_ada": nrm(ks[11], (DEPTH, 6 * D_MODEL), 0.02),
        "w_in": nrm(ks[12], (DEPTH, D_MODEL, N_IN), D_MODEL ** -0.5),
        "qn_g": gain(ks[13], (DEPTH, DH_A)),
        "kn_g": gain(ks[14], (DEPTH, DH_A)),
        "sb_bias": sb_bias,
        "w_gk2": nrm(ks[15], (DEPTH, GATE_RANK, H_B * DK_B), GATE_RANK ** -0.5),
        "b_gk2": nrm(ks[16], (DEPTH, H_B * DK_B), 0.1),
        "onorm_g": gain(ks[17], (DEPTH, DV_B)),
        "w_out": nrm(ks[18], (DEPTH, D_MODEL, D_MODEL), D_MODEL ** -0.5),
        "w_rg": nrm(ks[19], (DEPTH, D_MODEL, N_GROUPS), D_MODEL ** -0.5),
        "b_rg": nrm(ks[20], (DEPTH, N_GROUPS), 0.01),
        "w_re": nrm(ks[21], (DEPTH, D_MODEL, N_EXPERTS), D_MODEL ** -0.5),
        "b_re": nrm(ks[22], (DEPTH, N_EXPERTS), 0.01),
        "w_e_gate": nrm(ks[23], (DEPTH, N_EXPERTS, D_MODEL, D_EXPERT), D_MODEL ** -0.5),
        "w_e_up": nrm(ks[24], (DEPTH, N_EXPERTS, D_MODEL, D_EXPERT), D_MODEL ** -0.5),
        "w_e_down": nrm(ks[25], (DEPTH, N_EXPERTS, D_EXPERT, D_MODEL), D_EXPERT ** -0.5),
    }


def reference(x_prompt, x_sample, cache_k, cache_v, state_gla, page_table, c_prompt, c_sample,
              norm1_g, norm2_g, w_ada, b_ada, w_in, qn_g, kn_g, sb_bias, w_gk2, b_gk2, onorm_g, w_out,
              w_rg, b_rg, w_re, b_re, w_e_gate, w_e_up, w_e_down):
    dec_b, n_pages = page_table.shape
    past_len = n_pages * cache_k.shape[2]
    prompt_chunk = min(GLA_CHUNK, x_prompt.shape[1])
    sample_chunk = x_sample.shape[1]
    gla_zero = jnp.zeros((x_prompt.shape[0], H_B, DK_B, DV_B), jnp.float32)
    yp, ys = x_prompt, x_sample
    kp, vp, sp, ksm, vsm, ssm = [], [], [], [], [], []
    for l in range(DEPTH):
        lp = dict(norm1_g=norm1_g[l], norm2_g=norm2_g[l], w_ada=w_ada[l], b_ada=b_ada[l], w_in=w_in[l],
                  qn_g=qn_g[l], kn_g=kn_g[l], sb_bias=sb_bias[l], w_gk2=w_gk2[l], b_gk2=b_gk2[l],
                  onorm_g=onorm_g[l], w_out=w_out[l], w_rg=w_rg[l], b_rg=b_rg[l], w_re=w_re[l], b_re=b_re[l],
                  w_e_gate=w_e_gate[l], w_e_up=w_e_up[l], w_e_down=w_e_down[l])
        yp, k_r, v_r, s_r = trunk_layer(yp, c_prompt, lp, sb_prompt, gla_zero, prompt_chunk)
        kp.append(k_r)
        vp.append(v_r)
        sp.append(s_r.astype(state_gla.dtype))
        past_k = cache_k[l][page_table].reshape(dec_b, past_len, H_A, DH_A)
        past_v = cache_v[l][page_table].reshape(dec_b, past_len, H_A, DH_A)
        sb_fn = functools.partial(sb_sample, past_k=past_k, past_v=past_v)
        ys, k_r, v_r, s_r = trunk_layer(ys, c_sample, lp, sb_fn, state_gla[l], sample_chunk)
        ksm.append(k_r)
        vsm.append(v_r)
        ssm.append(s_r.astype(state_gla.dtype))
    return (yp, ys, jnp.stack(kp), jnp.stack(vp), jnp.stack(sp), jnp.stack(ksm), jnp.stack(vsm), jnp.stack(ssm))
```

```python
import functools

import jax
import jax.numpy as jnp
import numpy as np
from jax import lax
from jax.experimental import pallas as pl
from jax.experimental.pallas import tpu as pltpu

F32 = jnp.float32
BF16 = jnp.bfloat16

LANE = 128
SUBLANE = 8
VMEM_LIMIT = 56 * 1024 * 1024

D_MODEL = 1024
DH_A = 128
H_A = D_MODEL // DH_A
H_B = 4
DK_B = D_MODEL // (2 * H_B)
DV_B = D_MODEL // H_B
GATE_RANK = 16
GATE_NORM = 16.0
GLA_CHUNK = 64
N_GROUPS = 4
EXPERTS_PER_GROUP = 8
N_EXPERTS = N_GROUPS * EXPERTS_PER_GROUP
TOP_K = 2
D_EXPERT = D_MODEL // 4
EPS = 1e-6

COL_QA = 0
COL_KA = COL_QA + H_A * DH_A
COL_VA = COL_KA + H_A * DH_A
COL_QB = COL_VA + H_A * DH_A
COL_KB = COL_QB + H_B * DK_B
COL_VB = COL_KB + H_B * DK_B
COL_GA = COL_VB + H_B * DV_B
COL_GB = COL_GA + D_MODEL
COL_AL = COL_GB + D_MODEL
AL_PAD = 512
N_COLS = COL_AL + AL_PAD
PROJ_TN = 512
ROUTER_PAD = LANE


def _params(*sem):
    return pltpu.CompilerParams(dimension_semantics=sem, vmem_limit_bytes=VMEM_LIMIT)


def _softplus(z):
    return jnp.maximum(z, 0.0) + jnp.log1p(jnp.exp(-jnp.abs(z)))


def _split_bf16(x):
    hi = x.astype(BF16)
    lo = (x - hi.astype(F32)).astype(BF16)
    return hi, lo


def _dot(a, b):
    return jnp.dot(a, b, preferred_element_type=F32)


def _dot_nt(a, b):
    return lax.dot_general(a, b, (((1,), (1,)), ((), ())), preferred_element_type=F32)


def _dot_tn(a, b):
    return lax.dot_general(a, b, (((0,), (0,)), ((), ())), preferred_element_type=F32)


def _rms(x):
    return x * lax.rsqrt(jnp.mean(x * x, axis=-1, keepdims=True) + EPS)


def _ada_kernel(c_ref, w_ref, b_ref, o_ref):
    c = c_ref[...]
    a = (c * (1.0 / (1.0 + jnp.exp(-c)))).astype(BF16)
    o_ref[...] = _dot(a, w_ref[...].astype(BF16)) + b_ref[...]


def _ada(c, w_ada, b_ada):
    depth, d, n = w_ada.shape
    r = c.shape[0]
    tn = 1024
    return pl.pallas_call(
        _ada_kernel,
        grid=(depth, n // tn),
        in_specs=[
            pl.BlockSpec((r, d), lambda l, j: (0, 0)),
            pl.BlockSpec((None, d, tn), lambda l, j: (l, 0, j)),
            pl.BlockSpec((None, 1, tn), lambda l, j: (l, 0, j)),
        ],
        out_specs=pl.BlockSpec((None, r, tn), lambda l, j: (l, 0, j)),
        out_shape=jax.ShapeDtypeStruct((depth, r, n), F32),
        compiler_params=_params("parallel", "parallel"),
        name="ada",
    )(c, w_ada, b_ada.reshape(depth, 1, n))


def _proj_kernel(x_ref, g_ref, sc_ref, sh_ref, w_ref, cg_ref, o_ref, h_scr, *, n_norm_tiles):
    j = pl.program_id(1)

    @pl.when(j == 0)
    def _():
        h = _rms(x_ref[...]) * g_ref[...]
        h_scr[...] = (h * (1.0 + sc_ref[...]) + sh_ref[...]).astype(BF16)

    acc = _dot(h_scr[...], w_ref[...])

    @pl.when(j < n_norm_tiles)
    def _():
        for c in range(acc.shape[1] // DH_A):
            sl = slice(c * DH_A, (c + 1) * DH_A)
            o_ref[:, sl] = _rms(acc[:, sl]) * cg_ref[:, sl]

    @pl.when(j >= n_norm_tiles)
    def _():
        o_ref[...] = acc


def _proj(x, g, sc, sh, w, colgain, tm, rows_per_mod):
    n, d = x.shape
    r = sc.shape[1]
    tn = PROJ_TN
    mod_spec = pl.BlockSpec((None, r, d), lambda i, j: (i // rows_per_mod, 0, 0))
    return pl.pallas_call(
        functools.partial(_proj_kernel, n_norm_tiles=COL_VA // tn),
        grid=(n // tm, N_COLS // tn),
        in_specs=[
            pl.BlockSpec((tm, d), lambda i, j: (i, 0)),
            pl.BlockSpec((1, d), lambda i, j: (0, 0)),
            mod_spec,
            mod_spec,
            pl.BlockSpec((d, tn), lambda i, j: (0, j)),
            pl.BlockSpec((1, tn), lambda i, j: (0, j)),
        ],
        out_specs=pl.BlockSpec((tm, tn), lambda i, j: (i, j)),
        out_shape=jax.ShapeDtypeStruct((n, N_COLS), F32),
        scratch_shapes=[pltpu.VMEM((tm, d), BF16)],
        compiler_params=_params("parallel", "arbitrary"),
        name="proj",
    )(x, g, sc, sh, w, colgain)


def _sb_prompt_kernel(bias_ref, q_ref, k_ref, v_ref, u_ref, o_ref, acc_scr, run_scr, *, blk):
    h = pl.program_id(1)
    qi = pl.program_id(2)
    bias = bias_ref[h]
    q = q_ref[...].astype(BF16)
    u = u_ref[...]
    row = lax.broadcasted_iota(jnp.int32, (blk, blk), 0)
    col = lax.broadcasted_iota(jnp.int32, (blk, blk), 1)
    acc_scr[...] = jnp.zeros_like(acc_scr)
    run_scr[...] = jnp.zeros_like(run_scr)

    def body(jj, carry):
        j = qi - jj
        off = pl.multiple_of(j * blk, blk)
        k = k_ref[pl.ds(off, blk), :].astype(BF16)
        v = v_ref[pl.ds(off, blk), :].astype(BF16)
        z = _dot_nt(q, k) * (DH_A ** -0.5) + bias
        sp = _softplus(z)
        before = col < row + jj * blk
        log_keep = jnp.where(before, -sp, 0.0)
        hi, lo = _split_bf16(log_keep)
        between = _dot(hi, u) + _dot(lo, u) + run_scr[...]
        w = jnp.where(before, jnp.exp(z - sp + between), 0.0)
        acc_scr[...] += _dot(w.astype(BF16), v)
        run_scr[...] += jnp.sum(log_keep, axis=1, keepdims=True)
        return carry

    lax.fori_loop(0, qi + 1, body, 0)
    o_ref[...] = acc_scr[...]


def _strict_lower(n):
    r = np.arange(n)
    return jnp.asarray(r[:, None] > r[None, :], dtype=BF16)


def _sb_prompt(proj, bias, batch, seq, blk=256):
    nq = seq // blk
    return pl.pallas_call(
        functools.partial(_sb_prompt_kernel, blk=blk),
        grid_spec=pltpu.PrefetchScalarGridSpec(
            num_scalar_prefetch=1,
            grid=(batch, H_A, nq),
            in_specs=[
                pl.BlockSpec((blk, DH_A), lambda b, h, i, s: (b * nq + i, COL_QA // DH_A + h)),
                pl.BlockSpec((seq, DH_A), lambda b, h, i, s: (b, COL_KA // DH_A + h)),
                pl.BlockSpec((seq, DH_A), lambda b, h, i, s: (b, COL_VA // DH_A + h)),
                pl.BlockSpec((blk, blk), lambda b, h, i, s: (0, 0)),
            ],
            out_specs=pl.BlockSpec((blk, DH_A), lambda b, h, i, s: (b * nq + i, h)),
            scratch_shapes=[pltpu.VMEM((blk, DH_A), F32), pltpu.VMEM((blk, 1), F32)],
        ),
        out_shape=jax.ShapeDtypeStruct((batch * seq, D_MODEL), F32),
        compiler_params=_params("parallel", "parallel", "arbitrary"),
        name="sb_prompt",
    )(bias, proj, proj, proj, _strict_lower(blk))


def _sb_sample_kernel(pt_ref, qbd_ref, kn_ref, vn_ref, kc_ref, vc_ref, bias_ref, ut_ref, o_ref,
                      acc_scr, run_scr, kpad, vpad, *, t_new, page):
    p = pl.program_id(1)
    n_steps = pl.num_programs(1)
    ncol = t_new * H_A
    row = lax.broadcasted_iota(jnp.int32, (page, LANE), 0)
    col = lax.broadcasted_iota(jnp.int32, (page, LANE), 1)

    def tile(k, v, before):
        z = _dot(k, qbd_ref[...]) * (DH_A ** -0.5) + bias_ref[...]
        sp = _softplus(z)
        log_keep = -sp if before is None else jnp.where(before, -sp, 0.0)
        hi, lo = _split_bf16(log_keep)
        between = _dot(ut_ref[...], hi) + _dot(ut_ref[...], lo) + run_scr[...]
        w = jnp.exp(z - sp + between)
        if before is not None:
            w = jnp.where(before, w, 0.0)
        acc_scr[...] += _dot(w.T[:ncol].astype(BF16), v)
        run_scr[...] += jnp.sum(log_keep, axis=0, keepdims=True)

    @pl.when(p == 0)
    def _():
        acc_scr[...] = jnp.zeros_like(acc_scr)
        run_scr[...] = jnp.zeros_like(run_scr)
        kpad[...] = jnp.zeros_like(kpad)
        vpad[...] = jnp.zeros_like(vpad)
        kpad[0:SUBLANE, :] = kn_ref[...]
        vpad[0:SUBLANE, :] = vn_ref[...]
        tile(kpad[...].astype(BF16), vpad[...].astype(BF16), (row < col // H_A) & (row < t_new))

    @pl.when(p > 0)
    def _():
        tile(kc_ref[...].astype(BF16), vc_ref[...].astype(BF16), None)

    @pl.when(p == n_steps - 1)
    def _():
        r8 = lax.broadcasted_iota(jnp.int32, (H_A, D_MODEL), 0)
        c8 = lax.broadcasted_iota(jnp.int32, (H_A, D_MODEL), 1)
        diag = (c8 // DH_A) == r8
        for t in range(t_new):
            slab = acc_scr[t * H_A:(t + 1) * H_A, :]
            o_ref[t:t + 1, :] = jnp.sum(jnp.where(diag, slab, 0.0), axis=0, keepdims=True)


def _strict_upper(n):
    r = np.arange(n)
    return jnp.asarray(r[None, :] > r[:, None], dtype=BF16)


def _sb_sample(layer, q, k_new, v_new, cache_k, cache_v, page_table, bias):
    nb, t_new, d = q.shape
    n_pages = page_table.shape[1]
    page = cache_k.shape[2]
    assert page == LANE and t_new <= SUBLANE and t_new * H_A <= LANE
    ncol = t_new * H_A
    q4 = q.reshape(nb, t_new, H_A, DH_A)
    qbd = jnp.einsum("bthd,hg->bhdtg", q4, jnp.eye(H_A, dtype=F32)).reshape(nb, d, ncol)
    qbd = jnp.pad(qbd, ((0, 0), (0, 0), (0, LANE - ncol))).astype(BF16)
    bias_l = jnp.pad(jnp.tile(bias, t_new), (0, LANE - ncol)).reshape(1, LANE)
    pad_rows = ((0, 0), (0, SUBLANE - t_new), (0, 0))
    kn = jnp.pad(k_new, pad_rows)
    vn = jnp.pad(v_new, pad_rows)

    def page_idx(b, p, pt):
        return (layer, pt[b, n_pages - jnp.maximum(p, 1)], 0, 0)

    return pl.pallas_call(
        functools.partial(_sb_sample_kernel, t_new=t_new, page=page),
        grid_spec=pltpu.PrefetchScalarGridSpec(
            num_scalar_prefetch=1,
            grid=(nb, n_pages + 1),
            in_specs=[
                pl.BlockSpec((None, d, LANE), lambda b, p, pt: (b, 0, 0)),
                pl.BlockSpec((None, SUBLANE, d), lambda b, p, pt: (b, 0, 0)),
                pl.BlockSpec((None, SUBLANE, d), lambda b, p, pt: (b, 0, 0)),
                pl.BlockSpec((None, None, page, d), page_idx),
                pl.BlockSpec((None, None, page, d), page_idx),
                pl.BlockSpec((1, LANE), lambda b, p, pt: (0, 0)),
                pl.BlockSpec((page, page), lambda b, p, pt: (0, 0)),
            ],
            out_specs=pl.BlockSpec((None, t_new, d), lambda b, p, pt: (b, 0, 0)),
            scratch_shapes=[
                pltpu.VMEM((ncol, d), F32),
                pltpu.VMEM((1, LANE), F32),
                pltpu.VMEM((page, d), F32),
                pltpu.VMEM((page, d), F32),
            ],
        ),
        out_shape=jax.ShapeDtypeStruct((nb, t_new, d), F32),
        compiler_params=_params("parallel", "arbitrary"),
        name="sb_sample",
    )(page_table, qbd, kn, vn, cache_k, cache_v, bias_l, _strict_upper(page))


def _gla_kernel(q_ref, k_ref, v_ref, al_ref, wg_ref, bg_ref, on_ref, s0_ref, tri_ref, o_ref, sfin_ref,
                s_scr, *, chunk, t_valid):
    c = pl.program_id(1)

    @pl.when(c == 0)
    def _():
        s_scr[...] = s0_ref[...]

    gate = _dot(al_ref[...].astype(BF16), wg_ref[...]) + bg_ref[...]
    log_a = -_softplus(-gate) * (1.0 / GATE_NORM)
    if t_valid < chunk:
        rows = lax.broadcasted_iota(jnp.int32, log_a.shape, 0)
        log_a = jnp.where(rows < t_valid, log_a, 0.0)
    tri = tri_ref[...]
    lower = lax.broadcasted_iota(jnp.int32, (chunk, chunk), 0) >= lax.broadcasted_iota(jnp.int32, (chunk, chunk), 1)
    ones = jnp.ones((chunk, DK_B), BF16)
    for h in range(H_B):
        ks = slice(h * DK_B, (h + 1) * DK_B)
        vs = slice(h * DV_B, (h + 1) * DV_B)
        hi, lo = _split_bf16(log_a[:, ks])
        cum = _dot(tri, hi) + _dot(tri, lo)
        last = cum[chunk - 1:chunk, :]
        last_col = jnp.exp(_dot_tn(hi, ones) + _dot_tn(lo, ones))
        q = q_ref[:, ks] * (DK_B ** -0.5)
        k = k_ref[:, ks]
        v = v_ref[:, vs].astype(BF16)
        q_dec = (q * jnp.exp(cum)).astype(BF16)
        k_dec = (k * jnp.exp(-cum)).astype(BF16)
        att = jnp.where(lower, _dot_nt(q_dec, k_dec), 0.0)
        s = s_scr[h]
        o = _dot(q_dec, s.astype(BF16)) + _dot(att.astype(BF16), v)
        k_rem = (k * jnp.exp(last - cum)).astype(BF16)
        s_scr[h] = jnp.concatenate([last_col] * (DV_B // DK_B), axis=1) * s + _dot_tn(k_rem, v)
        o_ref[:, vs] = _rms(o) * on_ref[...]

    @pl.when(c == pl.num_programs(1) - 1)
    def _():
        sfin_ref[...] = s_scr[...]


def _gla(proj, w_gk2, b_gk2, onorm_g, s0, nb, seq, chunk, t_valid):
    nc = seq // chunk
    wg = jnp.zeros((LANE, H_B * DK_B), F32).at[:GATE_RANK].set(w_gk2).astype(BF16)
    r = np.arange(chunk)
    tri = jnp.asarray(r[:, None] >= r[None, :], dtype=BF16)
    kw = H_B * DK_B
    vw = H_B * DV_B
    state_spec = pl.BlockSpec((None, H_B, DK_B, DV_B), lambda b, c: (b, 0, 0, 0))
    return pl.pallas_call(
        functools.partial(_gla_kernel, chunk=chunk, t_valid=t_valid),
        grid=(nb, nc),
        in_specs=[
            pl.BlockSpec((chunk, kw), lambda b, c: (b * nc + c, COL_QB // kw)),
            pl.BlockSpec((chunk, kw), lambda b, c: (b * nc + c, COL_KB // kw)),
            pl.BlockSpec((chunk, vw), lambda b, c: (b * nc + c, COL_VB // vw)),
            pl.BlockSpec((chunk, LANE), lambda b, c: (b * nc + c, COL_AL // LANE)),
            pl.BlockSpec((LANE, kw), lambda b, c: (0, 0)),
            pl.BlockSpec((1, kw), lambda b, c: (0, 0)),
            pl.BlockSpec((1, DV_B), lambda b, c: (0, 0)),
            state_spec,
            pl.BlockSpec((chunk, chunk), lambda b, c: (0, 0)),
        ],
        out_specs=[
            pl.BlockSpec((chunk, vw), lambda b, c: (b * nc + c, 0)),
            state_spec,
        ],
        out_shape=[
            jax.ShapeDtypeStruct((nb * seq, vw), F32),
            jax.ShapeDtypeStruct((nb, H_B, DK_B, DV_B), F32),
        ],
        scratch_shapes=[pltpu.VMEM((H_B, DK_B, DV_B), F32)],
        compiler_params=_params("parallel", "arbitrary"),
        name="gla",
    )(proj, proj, proj, proj, wg, b_gk2.reshape(1, kw), onorm_g.reshape(1, DV_B), s0, tri)


def _mix_kernel(ga_ref, gb_ref, oa_ref, ob_ref, w_ref, x_ref, gt_ref, o_ref):
    def sig(t):
        return 1.0 / (1.0 + jnp.exp(-t))

    merged = sig(ga_ref[...]) * oa_ref[...] + sig(gb_ref[...]) * ob_ref[...]
    o_ref[...] = x_ref[...] + gt_ref[...] * _dot(merged.astype(BF16), w_ref[...])


def _mix_out(proj, o_a, o_b, w_out, x, gt, tm, rows_per_mod):
    n, d = x.shape
    r = gt.shape[1]
    row = lambda i: (i, 0)
    return pl.pallas_call(
        _mix_kernel,
        grid=(n // tm,),
        in_specs=[
            pl.BlockSpec((tm, d), lambda i: (i, COL_GA // d)),
            pl.BlockSpec((tm, d), lambda i: (i, COL_GB // d)),
            pl.BlockSpec((tm, d), row),
            pl.BlockSpec((tm, d), row),
            pl.BlockSpec((d, d), lambda i: (0, 0)),
            pl.BlockSpec((tm, d), row),
            pl.BlockSpec((None, r, d), lambda i: (i // rows_per_mod, 0, 0)),
        ],
        out_specs=pl.BlockSpec((tm, d), row),
        out_shape=jax.ShapeDtypeStruct((n, d), F32),
        compiler_params=_params("parallel"),
        name="mix_out",
    )(proj, proj, o_a, o_b, w_out, x, gt)


def _router_kernel(x_ref, g_ref, sc_ref, sh_ref, whi_ref, wlo_ref, b_ref, h_ref, lg_ref):
    h = _rms(x_ref[...]) * g_ref[...]
    h = h * (1.0 + sc_ref[...]) + sh_ref[...]
    h_ref[...] = h.astype(BF16)
    hi, lo = _split_bf16(h)
    lg_ref[...] = _dot(hi, whi_ref[...]) + (_dot(lo, whi_ref[...]) + _dot(hi, wlo_ref[...])) + b_ref[...]


def _router(x, g, sc, sh, w_r, b_r, tm, rows_per_mod):
    n, d = x.shape
    r = sc.shape[1]
    whi, wlo = _split_bf16(w_r)
    mod_spec = pl.BlockSpec((None, r, d), lambda i: (i // rows_per_mod, 0, 0))
    full = lambda i: (0, 0)
    return pl.pallas_call(
        _router_kernel,
        grid=(n // tm,),
        in_specs=[
            pl.BlockSpec((tm, d), lambda i: (i, 0)),
            pl.BlockSpec((1, d), full),
            mod_spec,
            mod_spec,
            pl.BlockSpec((d, ROUTER_PAD), full),
            pl.BlockSpec((d, ROUTER_PAD), full),
            pl.BlockSpec((1, ROUTER_PAD), full),
        ],
        out_specs=[
            pl.BlockSpec((tm, d), lambda i: (i, 0)),
            pl.BlockSpec((tm, ROUTER_PAD), lambda i: (i, 0)),
        ],
        out_shape=[
            jax.ShapeDtypeStruct((n, d), BF16),
            jax.ShapeDtypeStruct((n, ROUTER_PAD), F32),
        ],
        compiler_params=_params("parallel"),
        name="router",
    )(x, g, sc, sh, whi, wlo, b_r)


def _routing_weights(logits):
    n = logits.shape[0]
    g_logits = logits[:, :N_GROUPS]
    g_idx = jnp.argmax(g_logits, axis=-1)
    g_w = jnp.take_along_axis(jax.nn.softmax(g_logits, axis=-1), g_idx[:, None], axis=1)
    e_logits = logits[:, N_GROUPS:N_GROUPS + N_EXPERTS].reshape(n, N_GROUPS, EXPERTS_PER_GROUP)
    e_logits = jnp.take_along_axis(e_logits, g_idx[:, None, None], axis=1)[:, 0]
    top_v, top_i = lax.top_k(e_logits, TOP_K)
    weight = jax.nn.softmax(top_v, axis=-1) * g_w
    expert_id = g_idx[:, None] * EXPERTS_PER_GROUP + top_i
    onehot = expert_id[:, :, None] == jnp.arange(ROUTER_PAD)[None, None, :]
    return jnp.sum(jnp.where(onehot, weight[:, :, None], 0.0), axis=1)


def _moe_kernel(h_ref, wt_ref, wg_ref, wu_ref, wd_ref, x_ref, gt_ref, o_ref, acc_scr):
    e = pl.program_id(1)

    @pl.when(e == 0)
    def _():
        acc_scr[...] = jnp.zeros_like(acc_scr)

    h = h_ref[...]
    lane = lax.broadcasted_iota(jnp.int32, wt_ref.shape, 1)
    we = jnp.sum(jnp.where(lane == e, wt_ref[...], 0.0), axis=1, keepdims=True)
    a = _dot(h, wg_ref[...])
    hid = a * (1.0 / (1.0 + jnp.exp(-a))) * _dot(h, wu_ref[...])
    hid = jnp.where(we != 0.0, hid * we, 0.0)
    acc_scr[...] += _dot(hid.astype(BF16), wd_ref[...])

    @pl.when(e == pl.num_programs(1) - 1)
    def _():
        o_ref[...] = x_ref[...] + gt_ref[...] * acc_scr[...]


def _moe(h, wt, wg, wu, wd, x, gt, tm, rows_per_mod):
    n, d = x.shape
    r = gt.shape[1]
    row = lambda i, e: (i, 0)
    return pl.pallas_call(
        _moe_kernel,
        grid=(n // tm, N_EXPERTS),
        in_specs=[
            pl.BlockSpec((tm, d), row),
            pl.BlockSpec((tm, ROUTER_PAD), row),
            pl.BlockSpec((None, d, D_EXPERT), lambda i, e: (e, 0, 0)),
            pl.BlockSpec((None, d, D_EXPERT), lambda i, e: (e, 0, 0)),
            pl.BlockSpec((None, D_EXPERT, d), lambda i, e: (e, 0, 0)),
            pl.BlockSpec((tm, d), row),
            pl.BlockSpec((None, r, d), lambda i, e: (i // rows_per_mod, 0, 0)),
        ],
        out_specs=pl.BlockSpec((tm, d), row),
        out_shape=jax.ShapeDtypeStruct((n, d), F32),
        scratch_shapes=[pltpu.VMEM((tm, d), F32)],
        compiler_params=_params("parallel", "arbitrary"),
        name="moe",
    )(h, wt, wg, wu, wd, x, gt)


def _layer_weights(l, w_in, qn_g, kn_g, w_out, w_rg, b_rg, w_re, b_re, w_e_gate, w_e_up, w_e_down):
    d = w_in.shape[1]
    offs = np.cumsum([0, H_A * DH_A, H_A * DH_A, H_A * DH_A, H_B * DK_B, H_B * DK_B, H_B * DV_B,
                      GATE_RANK, D_MODEL, D_MODEL])
    wl = w_in[l]
    seg = [wl[:, offs[i]:offs[i + 1]] for i in range(9)]
    qa, ka, va, qb, kb, vb, al, ga, gb = seg
    al = jnp.pad(al, ((0, 0), (0, AL_PAD - GATE_RANK)))
    w_proj = jnp.concatenate([qa, ka, va, qb, kb, vb, ga, gb, al], axis=1).astype(BF16)
    colgain = jnp.concatenate([jnp.tile(qn_g[l], H_A), jnp.tile(kn_g[l], H_A),
                               jnp.ones((N_COLS - COL_VA,), F32)]).reshape(1, N_COLS)
    pad = ROUTER_PAD - N_GROUPS - N_EXPERTS
    w_r = jnp.pad(jnp.concatenate([w_rg[l], w_re[l]], axis=1), ((0, 0), (0, pad)))
    b_r = jnp.pad(jnp.concatenate([b_rg[l], b_re[l]]), (0, pad)).reshape(1, ROUTER_PAD)
    return dict(w_proj=w_proj, colgain=colgain, w_out=w_out[l].astype(BF16), w_r=w_r, b_r=b_r,
                wg=w_e_gate[l].astype(BF16), wu=w_e_up[l].astype(BF16), wd=w_e_down[l].astype(BF16))


def _trunk_layer(x, mod, lw, sb_fn, gla_fn, norm1_g, norm2_g, tm, rows_per_mod):
    sh1, sc1, gt1, sh2, sc2, gt2 = mod
    proj = _proj(x, norm1_g, sc1, sh1, lw["w_proj"], lw["colgain"], tm, rows_per_mod)
    o_a = sb_fn(proj)
    o_b, s_new = gla_fn(proj)
    tm_mix = min(tm, 512)
    x = _mix_out(proj, o_a, o_b, lw["w_out"], x, gt1, tm_mix, rows_per_mod * (tm // tm_mix))
    h, logits = _router(x, norm2_g, sc2, sh2, lw["w_r"], lw["b_r"], tm, rows_per_mod)
    x = _moe(h, _routing_weights(logits), lw["wg"], lw["wu"], lw["wd"], x, gt2, tm, rows_per_mod)
    return x, proj, s_new


def kernel(x_prompt, x_sample, cache_k, cache_v, state_gla, page_table, c_prompt, c_sample,
           norm1_g, norm2_g, w_ada, b_ada, w_in, qn_g, kn_g, sb_bias, w_gk2, b_gk2, onorm_g, w_out,
           w_rg, b_rg, w_re, b_re, w_e_gate, w_e_up, w_e_down):
    depth = w_in.shape[0]
    bp, seq, d = x_prompt.shape
    bs, t_s, _ = x_sample.shape
    n_pool, page = cache_k.shape[1], cache_k.shape[2]
    n_p, n_s = bp * seq, bs * t_s
    tm_p = min(1024, seq)
    assert seq % tm_p == 0 and n_s % SUBLANE == 0

    c_all = jnp.concatenate([c_prompt, c_sample], axis=0)
    r_pad = -c_all.shape[0] % SUBLANE
    mod_all = _ada(jnp.pad(c_all, ((0, r_pad), (0, 0))), w_ada, b_ada)

    ck = cache_k.reshape(depth, n_pool, page, d)
    cv = cache_v.reshape(depth, n_pool, page, d)
    gla_zero = jnp.zeros((bp, H_B, DK_B, DV_B), F32)

    yp = x_prompt.reshape(n_p, d)
    ys = x_sample.reshape(n_s, d)
    outs = [[] for _ in range(6)]
    for l in range(depth):
        lw = _layer_weights(l, w_in, qn_g, kn_g, w_out, w_rg, b_rg, w_re, b_re, w_e_gate, w_e_up, w_e_down)
        n1, n2 = norm1_g[l].reshape(1, d), norm2_g[l].reshape(1, d)
        mod_p = [m.reshape(bp, 1, d) for m in jnp.split(mod_all[l, :bp], 6, axis=-1)]
        mod_s = [jnp.repeat(m, t_s, axis=0).reshape(1, n_s, d)
                 for m in jnp.split(mod_all[l, bp:bp + bs], 6, axis=-1)]

        sb_p = functools.partial(_sb_prompt, bias=sb_bias[l], batch=bp, seq=seq)
        gla_p = functools.partial(_gla, w_gk2=w_gk2[l], b_gk2=b_gk2[l], onorm_g=onorm_g[l], s0=gla_zero,
                                  nb=bp, seq=seq, chunk=min(GLA_CHUNK, seq), t_valid=min(GLA_CHUNK, seq))
        yp, proj_p, s_p = _trunk_layer(yp, mod_p, lw, sb_p, gla_p, n1, n2, tm_p, seq // tm_p)
        outs[0].append(proj_p[:, COL_KA:COL_VA].reshape(bp, seq, H_A, DH_A))
        outs[1].append(proj_p[:, COL_VA:COL_QB].reshape(bp, seq, H_A, DH_A))
        outs[2].append(s_p)

        def sb_s(proj, l=l):
            p3 = proj.reshape(bs, t_s, N_COLS)
            o = _sb_sample(l, p3[:, :, COL_QA:COL_KA], p3[:, :, COL_KA:COL_VA], p3[:, :, COL_VA:COL_QB],
                           ck, cv, page_table, sb_bias[l])
            return o.reshape(n_s, d)

        def gla_s(proj, l=l):
            p3 = jnp.pad(proj.reshape(bs, t_s, N_COLS), ((0, 0), (0, SUBLANE - t_s), (0, 0)))
            o, s_new = _gla(p3.reshape(bs * SUBLANE, N_COLS), w_gk2[l], b_gk2[l], onorm_g[l], state_gla[l],
                            nb=bs, seq=SUBLANE, chunk=SUBLANE, t_valid=t_s)
            return o.reshape(bs, SUBLANE, d)[:, :t_s].reshape(n_s, d), s_new

        ys, proj_s, s_s = _trunk_layer(ys, mod_s, lw, sb_s, gla_s, n1, n2, n_s, 1)
        outs[3].append(proj_s[:, COL_KA:COL_VA].reshape(bs, t_s, H_A, DH_A))
        outs[4].append(proj_s[:, COL_VA:COL_QB].reshape(bs, t_s, H_A, DH_A))
        outs[5].append(s_s.astype(state_gla.dtype))

    return (yp.reshape(bp, seq, d), ys.reshape(bs, t_s, d), jnp.stack(outs[0]), jnp.stack(outs[1]),
            jnp.stack(outs[2]).astype(state_gla.dtype), jnp.stack(outs[3]), jnp.stack(outs[4]), jnp.stack(outs[5]))
```

```python
import functools

import jax
import jax.numpy as jnp
import numpy as np
from jax import lax
from jax.experimental import pallas as pl
from jax.experimental.pallas import tpu as pltpu

F32 = jnp.float32
BF16 = jnp.bfloat16

LANE = 128
SUBLANE = 8
VMEM_LIMIT = 56 * 1024 * 1024

D_MODEL = 1024
DH_A = 128
H_A = D_MODEL // DH_A
H_B = 4
DK_B = D_MODEL // (2 * H_B)
DV_B = D_MODEL // H_B
GATE_RANK = 16
GATE_NORM = 16.0
GLA_CHUNK = 64
N_GROUPS = 4
EXPERTS_PER_GROUP = 8
N_EXPERTS = N_GROUPS * EXPERTS_PER_GROUP
TOP_K = 2
D_EXPERT = D_MODEL // 4
EPS = 1e-6

COL_QA = 0
COL_KA = COL_QA + H_A * DH_A
COL_VA = COL_KA + H_A * DH_A
COL_QB = COL_VA + H_A * DH_A
COL_KB = COL_QB + H_B * DK_B
COL_VB = COL_KB + H_B * DK_B
COL_GA = COL_VB + H_B * DV_B
COL_GB = COL_GA + D_MODEL
COL_AL = COL_GB + D_MODEL
AL_PAD = 512
N_COLS = COL_AL + AL_PAD
PROJ_TN = 512
ROUTER_PAD = LANE


def _params(*sem):
    return pltpu.CompilerParams(dimension_semantics=sem, vmem_limit_bytes=VMEM_LIMIT)


def _softplus(z):
    return jnp.maximum(z, 0.0) + jnp.log1p(jnp.exp(-jnp.abs(z)))


def _split_bf16(x):
    hi = x.astype(BF16)
    lo = (x - hi.astype(F32)).astype(BF16)
    return hi, lo


def _dot(a, b):
    return jnp.dot(a, b, preferred_element_type=F32)


def _dot_nt(a, b):
    return lax.dot_general(a, b, (((1,), (1,)), ((), ())), preferred_element_type=F32)


def _dot_tn(a, b):
    return lax.dot_general(a, b, (((0,), (0,)), ((), ())), preferred_element_type=F32)


def _rms(x):
    return x * lax.rsqrt(jnp.mean(x * x, axis=-1, keepdims=True) + EPS)


def _ada_kernel(c_ref, w_ref, b_ref, o_ref):
    c = c_ref[...]
    a = (c * (1.0 / (1.0 + jnp.exp(-c)))).astype(BF16)
    o_ref[...] = _dot(a, w_ref[...].astype(BF16)) + b_ref[...]


def _ada(c, w_ada, b_ada):
    depth, d, n = w_ada.shape
    r = c.shape[0]
    tn = 1024
    return pl.pallas_call(
        _ada_kernel,
        grid=(depth, n // tn),
        in_specs=[
            pl.BlockSpec((r, d), lambda l, j: (0, 0)),
            pl.BlockSpec((None, d, tn), lambda l, j: (l, 0, j)),
            pl.BlockSpec((None, 1, tn), lambda l, j: (l, 0, j)),
        ],
        out_specs=pl.BlockSpec((None, r, tn), lambda l, j: (l, 0, j)),
        out_shape=jax.ShapeDtypeStruct((depth, r, n), F32),
        compiler_params=_params("parallel", "parallel"),
        name="ada",
    )(c, w_ada, b_ada.reshape(depth, 1, n))


def _proj_kernel(x_ref, g_ref, sc_ref, sh_ref, w_ref, cg_ref, o_ref, h_scr, *, n_norm_tiles):
    j = pl.program_id(1)

    @pl.when(j == 0)
    def _():
        h = _rms(x_ref[...]) * g_ref[...]
        h_scr[...] = (h * (1.0 + sc_ref[...]) + sh_ref[...]).astype(BF16)

    acc = _dot(h_scr[...], w_ref[...])

    @pl.when(j < n_norm_tiles)
    def _():
        for c in range(acc.shape[1] // DH_A):
            sl = slice(c * DH_A, (c + 1) * DH_A)
            o_ref[:, sl] = _rms(acc[:, sl]) * cg_ref[:, sl]

    @pl.when(j >= n_norm_tiles)
    def _():
        o_ref[...] = acc


def _proj(x, g, sc, sh, w, colgain, tm, rows_per_mod):
    n, d = x.shape
    r = sc.shape[1]
    tn = PROJ_TN
    mod_spec = pl.BlockSpec((None, r, d), lambda i, j: (i // rows_per_mod, 0, 0))
    return pl.pallas_call(
        functools.partial(_proj_kernel, n_norm_tiles=COL_VA // tn),
        grid=(n // tm, N_COLS // tn),
        in_specs=[
            pl.BlockSpec((tm, d), lambda i, j: (i, 0)),
            pl.BlockSpec((1, d), lambda i, j: (0, 0)),
            mod_spec,
            mod_spec,
            pl.BlockSpec((d, tn), lambda i, j: (0, j)),
            pl.BlockSpec((1, tn), lambda i, j: (0, j)),
        ],
        out_specs=pl.BlockSpec((tm, tn), lambda i, j: (i, j)),
        out_shape=jax.ShapeDtypeStruct((n, N_COLS), F32),
        scratch_shapes=[pltpu.VMEM((tm, d), BF16)],
        compiler_params=_params("parallel", "arbitrary"),
        name="proj",
    )(x, g, sc, sh, w, colgain)


SB_HEADS_PER_STEP = 2


SB_TQ = 512
SB_TK = 256
LOG2E = 1.4426950408889634
SB_QSCALE = DH_A ** -0.5 * LOG2E


def _sb_softplus2(z):
    neg_abs = lax.bitcast_convert_type(lax.bitcast_convert_type(z, jnp.uint32) | jnp.uint32(0x80000000), F32)
    return jnp.maximum(z, 0.0) + jnp.log(1.0 + jnp.exp2(neg_abs)) * LOG2E


def _sb_weights(z, sp, drop, u, run):
    hi, lo = _split_bf16(drop)
    return jnp.exp2(z - sp - (_dot(hi, u) + _dot(lo, u)) - run)


def _sb_prompt_kernel(bias_ref, q_ref, k_ref, v_ref, u_ref, o_ref, run_scr, *, tq, tk, nh):
    hg = pl.program_id(1)
    qi = pl.program_id(2)
    u = u_ref[...]
    heads = [slice(i * DH_A, (i + 1) * DH_A) for i in range(nh)]
    qs = [(q_ref[:, hs] * SB_QSCALE).astype(BF16) for hs in heads]
    bias = [bias_ref[hg * nh + i] * LOG2E for i in range(nh)]
    row = lax.broadcasted_iota(jnp.int32, (tq, tk), 0)
    col = lax.broadcasted_iota(jnp.int32, (tq, tk), 1)
    o_ref[...] = jnp.zeros_like(o_ref)
    run_scr[...] = jnp.zeros_like(run_scr)

    def block(off, visible):
        for i, hs in enumerate(heads):
            k = k_ref[pl.ds(off, tk), hs].astype(BF16)
            v = v_ref[pl.ds(off, tk), hs].astype(BF16)
            z = _dot_nt(qs[i], k) + bias[i]
            sp = _sb_softplus2(z)
            drop = sp if visible is None else jnp.where(visible, sp, 0.0)
            w = _sb_weights(z, sp, drop, u, run_scr[i])
            if visible is not None:
                w = jnp.where(visible, w, 0.0)
            o_ref[:, hs] += _dot(w.astype(BF16), v)
            run_scr[i] += jnp.sum(drop, axis=1, keepdims=True)

    for d in reversed(range(tq // tk)):
        block(pl.multiple_of(qi * tq + d * tk, tk), col + d * tk < row)

    def body(jj, carry):
        block(pl.multiple_of(qi * tq - (jj + 1) * tk, tk), None)
        return carry

    lax.fori_loop(0, qi * (tq // tk), body, 0)


def _strict_lower(n):
    r = np.arange(n)
    return jnp.asarray(r[:, None] > r[None, :], dtype=BF16)


def _sb_prompt(proj, bias, batch, seq):
    tq, tk = min(SB_TQ, seq), SB_TK
    nq = seq // tq
    nh = SB_HEADS_PER_STEP
    w = nh * DH_A
    return pl.pallas_call(
        functools.partial(_sb_prompt_kernel, tq=tq, tk=tk, nh=nh),
        grid_spec=pltpu.PrefetchScalarGridSpec(
            num_scalar_prefetch=1,
            grid=(batch, H_A // nh, nq),
            in_specs=[
                pl.BlockSpec((tq, w), lambda b, h, i, s: (b * nq + i, COL_QA // w + h)),
                pl.BlockSpec((seq, w), lambda b, h, i, s: (b, COL_KA // w + h)),
                pl.BlockSpec((seq, w), lambda b, h, i, s: (b, COL_VA // w + h)),
                pl.BlockSpec((tk, tk), lambda b, h, i, s: (0, 0)),
            ],
            out_specs=pl.BlockSpec((tq, w), lambda b, h, i, s: (b * nq + i, h)),
            scratch_shapes=[pltpu.VMEM((nh, tq, 1), F32)],
        ),
        out_shape=jax.ShapeDtypeStruct((batch * seq, D_MODEL), F32),
        compiler_params=_params("parallel", "parallel", "arbitrary"),
        name="sb_prompt",
    )(bias, proj, proj, proj, _strict_lower(tk))


def _sb_sample_kernel(pt_ref, q_ref, kn_ref, vn_ref, *rest, n_pages, page):
    kc_refs, vc_refs = rest[:n_pages], rest[n_pages:2 * n_pages]
    bias_ref, u_ref, o_ref, kpad, vpad = rest[2 * n_pages:]
    u = u_ref[...]
    bias = bias_ref[...]
    heads = [slice(h * DH_A, (h + 1) * DH_A) for h in range(H_A)]
    qs = [(q_ref[:, hs] * SB_QSCALE).astype(BF16) for hs in heads]
    rows = H_A * SUBLANE

    def block(k_of, v_of, before, run, acc):
        z = jnp.concatenate([_dot_nt(q, k_of(h)) for h, q in enumerate(qs)], axis=0) + bias
        sp = _sb_softplus2(z)
        drop = sp if before is None else jnp.where(before, sp, 0.0)
        w = _sb_weights(z, sp, drop, u, run)
        if before is not None:
            w = jnp.where(before, w, 0.0)
        acc = [acc[h] + _dot(w[h * SUBLANE:(h + 1) * SUBLANE].astype(BF16), v_of(h)) for h in range(H_A)]
        return run + jnp.sum(drop, axis=1, keepdims=True), acc

    kpad[...] = jnp.zeros_like(kpad)
    vpad[...] = jnp.zeros_like(vpad)
    kpad[0:SUBLANE, :] = kn_ref[...]
    vpad[0:SUBLANE, :] = vn_ref[...]
    t_of_row = lax.broadcasted_iota(jnp.int32, (rows, page), 0) & (SUBLANE - 1)
    before = lax.broadcasted_iota(jnp.int32, (rows, page), 1) < t_of_row
    run = jnp.zeros((rows, 1), F32)
    acc = [jnp.zeros((SUBLANE, DH_A), F32) for _ in range(H_A)]
    run, acc = block(lambda h: kpad[:, heads[h]].astype(BF16), lambda h: vpad[:, heads[h]].astype(BF16),
                     before, run, acc)
    for p in reversed(range(n_pages)):
        kc, vc = kc_refs[p], vc_refs[p]
        run, acc = block(lambda h: kc[pl.ds(h, page, stride=H_A), :].astype(BF16),
                         lambda h: vc[pl.ds(h, page, stride=H_A), :].astype(BF16), None, run, acc)
    o_ref[...] = jnp.concatenate(acc, axis=1)[0:o_ref.shape[0]]


def _sb_sample(layer, q, k_new, v_new, cache_k, cache_v, page_table, bias):
    nb, t_new, d = q.shape
    n_pages = page_table.shape[1]
    page = cache_k.shape[2] // H_A
    assert page == LANE and t_new <= SUBLANE
    pad_rows = ((0, 0), (0, SUBLANE - t_new), (0, 0))
    rows = H_A * SUBLANE
    bias_t = jnp.broadcast_to(jnp.repeat(bias * LOG2E, SUBLANE)[:, None], (rows, page))
    row_spec = pl.BlockSpec((None, SUBLANE, d), lambda b, pt: (b, 0, 0))

    def page_spec(p):
        return pl.BlockSpec((None, None, page * H_A, DH_A), lambda b, pt: (layer, pt[b, p], 0, 0))

    pages = [page_spec(p) for p in range(n_pages)]
    return pl.pallas_call(
        functools.partial(_sb_sample_kernel, n_pages=n_pages, page=page),
        grid_spec=pltpu.PrefetchScalarGridSpec(
            num_scalar_prefetch=1,
            grid=(nb,),
            in_specs=[row_spec, row_spec, row_spec] + pages + pages + [
                pl.BlockSpec((rows, page), lambda b, pt: (0, 0)),
                pl.BlockSpec((page, page), lambda b, pt: (0, 0)),
            ],
            out_specs=pl.BlockSpec((None, t_new, d), lambda b, pt: (b, 0, 0)),
            scratch_shapes=[pltpu.VMEM((page, d), F32), pltpu.VMEM((page, d), F32)],
        ),
        out_shape=jax.ShapeDtypeStruct((nb, t_new, d), F32),
        compiler_params=_params("parallel"),
        name="sb_sample",
    )(page_table, jnp.pad(q, pad_rows), jnp.pad(k_new, pad_rows), jnp.pad(v_new, pad_rows),
      *([cache_k] * n_pages), *([cache_v] * n_pages), bias_t, _strict_lower(page))


def _gla_kernel(q_ref, k_ref, v_ref, al_ref, wg_ref, bg_ref, on_ref, s0_ref, tri_ref, o_ref, sfin_ref,
                s_scr, *, chunk, t_valid):
    c = pl.program_id(1)

    @pl.when(c == 0)
    def _():
        s_scr[...] = s0_ref[...]

    tri = tri_ref[...]
    lower = lax.broadcasted_iota(jnp.int32, (chunk, chunk), 0) >= lax.broadcasted_iota(jnp.int32, (chunk, chunk), 1)
    ones = jnp.ones((chunk, DK_B), BF16)
    for b in range(s_scr.shape[0]):
        gate = _dot(al_ref[b].astype(BF16), wg_ref[...]) + bg_ref[...]
        log_a = -_softplus(-gate) * (1.0 / GATE_NORM)
        if t_valid < chunk:
            rows = lax.broadcasted_iota(jnp.int32, log_a.shape, 0)
            log_a = jnp.where(rows < t_valid, log_a, 0.0)
        for h in range(H_B):
            ks = slice(h * DK_B, (h + 1) * DK_B)
            vs = slice(h * DV_B, (h + 1) * DV_B)
            hi, lo = _split_bf16(log_a[:, ks])
            cum = _dot(tri, hi) + _dot(tri, lo)
            last = cum[chunk - 1:chunk, :]
            last_col = jnp.exp(_dot_tn(hi, ones) + _dot_tn(lo, ones))
            q = q_ref[b, :, ks] * (DK_B ** -0.5)
            k = k_ref[b, :, ks]
            v = v_ref[b, :, vs].astype(BF16)
            q_dec = (q * jnp.exp(cum)).astype(BF16)
            k_dec = (k * jnp.exp(-cum)).astype(BF16)
            att = jnp.where(lower, _dot_nt(q_dec, k_dec), 0.0)
            s = s_scr[b, h]
            o = _dot(q_dec, s.astype(BF16)) + _dot(att.astype(BF16), v)
            k_rem = (k * jnp.exp(last - cum)).astype(BF16)
            s_scr[b, h] = jnp.concatenate([last_col] * (DV_B // DK_B), axis=1) * s + _dot_tn(k_rem, v)
            o_ref[b, :, vs] = _rms(o) * on_ref[...]

    @pl.when(c == pl.num_programs(1) - 1)
    def _():
        sfin_ref[...] = s_scr[...]


GLA_SEQS_PER_STEP = 4


def _gla(proj, w_gk2, b_gk2, onorm_g, s0, nb, seq, chunk, t_valid):
    nc = seq // chunk
    ns = GLA_SEQS_PER_STEP
    assert nb % ns == 0
    wg = jnp.zeros((LANE, H_B * DK_B), F32).at[:GATE_RANK].set(w_gk2).astype(BF16)
    r = np.arange(chunk)
    tri = jnp.asarray(r[:, None] >= r[None, :], dtype=BF16)
    kw = H_B * DK_B
    vw = H_B * DV_B
    p3 = proj.reshape(nb, seq, N_COLS)
    state_spec = pl.BlockSpec((ns, H_B, DK_B, DV_B), lambda b, c: (b, 0, 0, 0))
    o, s_fin = pl.pallas_call(
        functools.partial(_gla_kernel, chunk=chunk, t_valid=t_valid),
        grid=(nb // ns, nc),
        in_specs=[
            pl.BlockSpec((ns, chunk, kw), lambda b, c: (b, c, COL_QB // kw)),
            pl.BlockSpec((ns, chunk, kw), lambda b, c: (b, c, COL_KB // kw)),
            pl.BlockSpec((ns, chunk, vw), lambda b, c: (b, c, COL_VB // vw)),
            pl.BlockSpec((ns, chunk, LANE), lambda b, c: (b, c, COL_AL // LANE)),
            pl.BlockSpec((LANE, kw), lambda b, c: (0, 0)),
            pl.BlockSpec((1, kw), lambda b, c: (0, 0)),
            pl.BlockSpec((1, DV_B), lambda b, c: (0, 0)),
            state_spec,
            pl.BlockSpec((chunk, chunk), lambda b, c: (0, 0)),
        ],
        out_specs=[
            pl.BlockSpec((ns, chunk, vw), lambda b, c: (b, c, 0)),
            state_spec,
        ],
        out_shape=[
            jax.ShapeDtypeStruct((nb, seq, vw), F32),
            jax.ShapeDtypeStruct((nb, H_B, DK_B, DV_B), F32),
        ],
        scratch_shapes=[pltpu.VMEM((ns, H_B, DK_B, DV_B), F32)],
        compiler_params=_params("parallel", "arbitrary"),
        name="gla",
    )(p3, p3, p3, p3, wg, b_gk2.reshape(1, kw), onorm_g.reshape(1, DV_B), s0, tri)
    return o.reshape(nb * seq, vw), s_fin


def _mix_kernel(ga_ref, gb_ref, oa_ref, ob_ref, w_ref, x_ref, gt_ref, o_ref):
    def sig(t):
        return 1.0 / (1.0 + jnp.exp(-t))

    merged = sig(ga_ref[...]) * oa_ref[...] + sig(gb_ref[...]) * ob_ref[...]
    o_ref[...] = x_ref[...] + gt_ref[...] * _dot(merged.astype(BF16), w_ref[...])


def _mix_out(proj, o_a, o_b, w_out, x, gt, tm, rows_per_mod):
    n, d = x.shape
    r = gt.shape[1]
    row = lambda i: (i, 0)
    return pl.pallas_call(
        _mix_kernel,
        grid=(n // tm,),
        in_specs=[
            pl.BlockSpec((tm, d), lambda i: (i, COL_GA // d)),
            pl.BlockSpec((tm, d), lambda i: (i, COL_GB // d)),
            pl.BlockSpec((tm, d), row),
            pl.BlockSpec((tm, d), row),
            pl.BlockSpec((d, d), lambda i: (0, 0)),
            pl.BlockSpec((tm, d), row),
            pl.BlockSpec((None, r, d), lambda i: (i // rows_per_mod, 0, 0)),
        ],
        out_specs=pl.BlockSpec((tm, d), row),
        out_shape=jax.ShapeDtypeStruct((n, d), F32),
        compiler_params=_params("parallel"),
        name="mix_out",
    )(proj, proj, o_a, o_b, w_out, x, gt)


def _router_kernel(x_ref, g_ref, sc_ref, sh_ref, whi_ref, wlo_ref, b_ref, h_ref, lg_ref):
    h = _rms(x_ref[...]) * g_ref[...]
    h = h * (1.0 + sc_ref[...]) + sh_ref[...]
    h_ref[...] = h.astype(BF16)
    hi, lo = _split_bf16(h)
    lg_ref[...] = _dot(hi, whi_ref[...]) + (_dot(lo, whi_ref[...]) + _dot(hi, wlo_ref[...])) + b_ref[...]


def _router(x, g, sc, sh, w_r, b_r, tm, rows_per_mod):
    n, d = x.shape
    r = sc.shape[1]
    whi, wlo = _split_bf16(w_r)
    mod_spec = pl.BlockSpec((None, r, d), lambda i: (i // rows_per_mod, 0, 0))
    full = lambda i: (0, 0)
    return pl.pallas_call(
        _router_kernel,
        grid=(n // tm,),
        in_specs=[
            pl.BlockSpec((tm, d), lambda i: (i, 0)),
            pl.BlockSpec((1, d), full),
            mod_spec,
            mod_spec,
            pl.BlockSpec((d, ROUTER_PAD), full),
            pl.BlockSpec((d, ROUTER_PAD), full),
            pl.BlockSpec((1, ROUTER_PAD), full),
        ],
        out_specs=[
            pl.BlockSpec((tm, d), lambda i: (i, 0)),
            pl.BlockSpec((tm, ROUTER_PAD), lambda i: (i, 0)),
        ],
        out_shape=[
            jax.ShapeDtypeStruct((n, d), BF16),
            jax.ShapeDtypeStruct((n, ROUTER_PAD), F32),
        ],
        compiler_params=_params("parallel"),
        name="router",
    )(x, g, sc, sh, whi, wlo, b_r)


def _routing_weights(logits):
    n = logits.shape[0]
    g_logits = logits[:, :N_GROUPS]
    g_idx = jnp.argmax(g_logits, axis=-1)
    g_w = jnp.take_along_axis(jax.nn.softmax(g_logits, axis=-1), g_idx[:, None], axis=1)
    e_logits = logits[:, N_GROUPS:N_GROUPS + N_EXPERTS].reshape(n, N_GROUPS, EXPERTS_PER_GROUP)
    e_logits = jnp.take_along_axis(e_logits, g_idx[:, None, None], axis=1)[:, 0]
    top_v, top_i = lax.top_k(e_logits, TOP_K)
    weight = jax.nn.softmax(top_v, axis=-1) * g_w
    expert_id = g_idx[:, None] * EXPERTS_PER_GROUP + top_i
    onehot = expert_id[:, :, None] == jnp.arange(ROUTER_PAD)[None, None, :]
    return jnp.sum(jnp.where(onehot, weight[:, :, None], 0.0), axis=1)


def _moe_kernel(h_ref, wt_ref, wg_ref, wu_ref, wd_ref, x_ref, gt_ref, o_ref, acc_scr):
    e = pl.program_id(1)

    @pl.when(e == 0)
    def _():
        acc_scr[...] = jnp.zeros_like(acc_scr)

    h = h_ref[...]
    lane = lax.broadcasted_iota(jnp.int32, wt_ref.shape, 1)
    we = jnp.sum(jnp.where(lane == e, wt_ref[...], 0.0), axis=1, keepdims=True)
    a = _dot(h, wg_ref[...])
    hid = a * (1.0 / (1.0 + jnp.exp(-a))) * _dot(h, wu_ref[...])
    hid = jnp.where(we != 0.0, hid * we, 0.0)
    acc_scr[...] += _dot(hid.astype(BF16), wd_ref[...])

    @pl.when(e == pl.num_programs(1) - 1)
    def _():
        o_ref[...] = x_ref[...] + gt_ref[...] * acc_scr[...]


def _moe(h, wt, wg, wu, wd, x, gt, tm, rows_per_mod):
    n, d = x.shape
    r = gt.shape[1]
    row = lambda i, e: (i, 0)
    return pl.pallas_call(
        _moe_kernel,
        grid=(n // tm, N_EXPERTS),
        in_specs=[
            pl.BlockSpec((tm, d), row),
            pl.BlockSpec((tm, ROUTER_PAD), row),
            pl.BlockSpec((None, d, D_EXPERT), lambda i, e: (e, 0, 0)),
            pl.BlockSpec((None, d, D_EXPERT), lambda i, e: (e, 0, 0)),
            pl.BlockSpec((None, D_EXPERT, d), lambda i, e: (e, 0, 0)),
            pl.BlockSpec((tm, d), row),
            pl.BlockSpec((None, r, d), lambda i, e: (i // rows_per_mod, 0, 0)),
        ],
        out_specs=pl.BlockSpec((tm, d), row),
        out_shape=jax.ShapeDtypeStruct((n, d), F32),
        scratch_shapes=[pltpu.VMEM((tm, d), F32)],
        compiler_params=_params("parallel", "arbitrary"),
        name="moe",
    )(h, wt, wg, wu, wd, x, gt)


def _layer_weights(l, w_in, qn_g, kn_g, w_out, w_rg, b_rg, w_re, b_re, w_e_gate, w_e_up, w_e_down):
    d = w_in.shape[1]
    offs = np.cumsum([0, H_A * DH_A, H_A * DH_A, H_A * DH_A, H_B * DK_B, H_B * DK_B, H_B * DV_B,
                      GATE_RANK, D_MODEL, D_MODEL])
    wl = w_in[l]
    seg = [wl[:, offs[i]:offs[i + 1]] for i in range(9)]
    qa, ka, va, qb, kb, vb, al, ga, gb = seg
    al = jnp.pad(al, ((0, 0), (0, AL_PAD - GATE_RANK)))
    w_proj = jnp.concatenate([qa, ka, va, qb, kb, vb, ga, gb, al], axis=1).astype(BF16)
    colgain = jnp.concatenate([jnp.tile(qn_g[l], H_A), jnp.tile(kn_g[l], H_A),
                               jnp.ones((N_COLS - COL_VA,), F32)]).reshape(1, N_COLS)
    pad = ROUTER_PAD - N_GROUPS - N_EXPERTS
    w_r = jnp.pad(jnp.concatenate([w_rg[l], w_re[l]], axis=1), ((0, 0), (0, pad)))
    b_r = jnp.pad(jnp.concatenate([b_rg[l], b_re[l]]), (0, pad)).reshape(1, ROUTER_PAD)
    return dict(w_proj=w_proj, colgain=colgain, w_out=w_out[l].astype(BF16), w_r=w_r, b_r=b_r,
                wg=w_e_gate[l].astype(BF16), wu=w_e_up[l].astype(BF16), wd=w_e_down[l].astype(BF16))


def _trunk_layer(x, mod, lw, sb_fn, gla_fn, norm1_g, norm2_g, tm, rows_per_mod):
    sh1, sc1, gt1, sh2, sc2, gt2 = mod
    proj = _proj(x, norm1_g, sc1, sh1, lw["w_proj"], lw["colgain"], tm, rows_per_mod)
    o_a = sb_fn(proj)
    o_b, s_new = gla_fn(proj)
    tm_mix = min(tm, 512)
    x = _mix_out(proj, o_a, o_b, lw["w_out"], x, gt1, tm_mix, rows_per_mod * (tm // tm_mix))
    h, logits = _router(x, norm2_g, sc2, sh2, lw["w_r"], lw["b_r"], tm, rows_per_mod)
    x = _moe(h, _routing_weights(logits), lw["wg"], lw["wu"], lw["wd"], x, gt2, tm, rows_per_mod)
    return x, proj, s_new


def kernel(x_prompt, x_sample, cache_k, cache_v, state_gla, page_table, c_prompt, c_sample,
           norm1_g, norm2_g, w_ada, b_ada, w_in, qn_g, kn_g, sb_bias, w_gk2, b_gk2, onorm_g, w_out,
           w_rg, b_rg, w_re, b_re, w_e_gate, w_e_up, w_e_down):
    depth = w_in.shape[0]
    bp, seq, d = x_prompt.shape
    bs, t_s, _ = x_sample.shape
    n_pool, page = cache_k.shape[1], cache_k.shape[2]
    n_p, n_s = bp * seq, bs * t_s
    tm_p = min(1024, seq)
    assert seq % tm_p == 0 and n_s % SUBLANE == 0

    c_all = jnp.concatenate([c_prompt, c_sample], axis=0)
    r_pad = -c_all.shape[0] % SUBLANE
    mod_all = _ada(jnp.pad(c_all, ((0, r_pad), (0, 0))), w_ada, b_ada)

    ck = cache_k.reshape(depth, n_pool, page * H_A, DH_A)
    cv = cache_v.reshape(depth, n_pool, page * H_A, DH_A)
    gla_zero = jnp.zeros((bp, H_B, DK_B, DV_B), F32)

    yp = x_prompt.reshape(n_p, d)
    ys = x_sample.reshape(n_s, d)
    outs = [[] for _ in range(6)]
    for l in range(depth):
        lw = _layer_weights(l, w_in, qn_g, kn_g, w_out, w_rg, b_rg, w_re, b_re, w_e_gate, w_e_up, w_e_down)
        n1, n2 = norm1_g[l].reshape(1, d), norm2_g[l].reshape(1, d)
        mod_p = [m.reshape(bp, 1, d) for m in jnp.split(mod_all[l, :bp], 6, axis=-1)]
        mod_s = [jnp.repeat(m, t_s, axis=0).reshape(1, n_s, d)
                 for m in jnp.split(mod_all[l, bp:bp + bs], 6, axis=-1)]

        sb_p = functools.partial(_sb_prompt, bias=sb_bias[l], batch=bp, seq=seq)
        gla_p = functools.partial(_gla, w_gk2=w_gk2[l], b_gk2=b_gk2[l], onorm_g=onorm_g[l], s0=gla_zero,
                                  nb=bp, seq=seq, chunk=min(GLA_CHUNK, seq), t_valid=min(GLA_CHUNK, seq))
        yp, proj_p, s_p = _trunk_layer(yp, mod_p, lw, sb_p, gla_p, n1, n2, tm_p, seq // tm_p)
        outs[0].append(proj_p[:, COL_KA:COL_VA].reshape(bp, seq, H_A, DH_A))
        outs[1].append(proj_p[:, COL_VA:COL_QB].reshape(bp, seq, H_A, DH_A))
        outs[2].append(s_p)

        def sb_s(proj, l=l):
            p3 = proj.reshape(bs, t_s, N_COLS)
            o = _sb_sample(l, p3[:, :, COL_QA:COL_KA], p3[:, :, COL_KA:COL_VA], p3[:, :, COL_VA:COL_QB],
                           ck, cv, page_table, sb_bias[l])
            return o.reshape(n_s, d)

        def gla_s(proj, l=l):
            p3 = jnp.pad(proj.reshape(bs, t_s, N_COLS), ((0, 0), (0, SUBLANE - t_s), (0, 0)))
            o, s_new = _gla(p3.reshape(bs * SUBLANE, N_COLS), w_gk2[l], b_gk2[l], onorm_g[l], state_gla[l],
                            nb=bs, seq=SUBLANE, chunk=SUBLANE, t_valid=t_s)
            return o.reshape(bs, SUBLANE, d)[:, :t_s].reshape(n_s, d), s_new

        ys, proj_s, s_s = _trunk_layer(ys, mod_s, lw, sb_s, gla_s, n1, n2, n_s, 1)
        outs[3].append(proj_s[:, COL_KA:COL_VA].reshape(bs, t_s, H_A, DH_A))
        outs[4].append(proj_s[:, COL_VA:COL_QB].reshape(bs, t_s, H_A, DH_A))
        outs[5].append(s_s.astype(state_gla.dtype))

    return (yp.reshape(bp, seq, d), ys.reshape(bs, t_s, d), jnp.stack(outs[0]), jnp.stack(outs[1]),
            jnp.stack(outs[2]).astype(state_gla.dtype), jnp.stack(outs[3]), jnp.stack(outs[4]), jnp.stack(outs[5]))
```

```python
import functools

import jax
import jax.numpy as jnp
import numpy as np
from jax import lax
from jax.experimental import pallas as pl
from jax.experimental.pallas import tpu as pltpu

F32 = jnp.float32
BF16 = jnp.bfloat16

LANE = 128
SUBLANE = 8
VMEM_LIMIT = 56 * 1024 * 1024

D_MODEL = 1024
DH_A = 128
H_A = D_MODEL // DH_A
H_B = 4
DK_B = D_MODEL // (2 * H_B)
DV_B = D_MODEL // H_B
GATE_RANK = 16
GATE_NORM = 16.0
GLA_CHUNK = 64
N_GROUPS = 4
EXPERTS_PER_GROUP = 8
N_EXPERTS = N_GROUPS * EXPERTS_PER_GROUP
TOP_K = 2
D_EXPERT = D_MODEL // 4
EPS = 1e-6

COL_QA = 0
COL_QB = COL_QA + H_A * DH_A
COL_KB = COL_QB + H_B * DK_B
COL_VB = COL_KB + H_B * DK_B
COL_GA = COL_VB + H_B * DV_B
COL_GB = COL_GA + D_MODEL
COL_AL = COL_GB + D_MODEL
AL_PAD = 512
N_COLS = COL_AL + AL_PAD
PROJ_TN = 512
PROJ_ROW_SCALE = 2
ROUTER_PAD = LANE


def _params(*sem):
    return pltpu.CompilerParams(dimension_semantics=sem, vmem_limit_bytes=VMEM_LIMIT)


def _softplus(z):
    return jnp.maximum(z, 0.0) + jnp.log1p(jnp.exp(-jnp.abs(z)))


def _split_bf16(x):
    hi = x.astype(BF16)
    lo = (x - hi.astype(F32)).astype(BF16)
    return hi, lo


def _dot(a, b):
    return jnp.dot(a, b, preferred_element_type=F32)


def _dot_nt(a, b):
    return lax.dot_general(a, b, (((1,), (1,)), ((), ())), preferred_element_type=F32)


def _dot_tn(a, b):
    return lax.dot_general(a, b, (((0,), (0,)), ((), ())), preferred_element_type=F32)


def _rms(x):
    return x * lax.rsqrt(jnp.mean(x * x, axis=-1, keepdims=True) + EPS)


def _ada_kernel(c_ref, w_ref, b_ref, o_ref):
    c = c_ref[...]
    a = (c * (1.0 / (1.0 + jnp.exp(-c)))).astype(BF16)
    o_ref[...] = _dot(a, w_ref[...].astype(BF16)) + b_ref[...]


def _ada(c, w_ada, b_ada):
    depth, d, n = w_ada.shape
    r = c.shape[0]
    tn = 1024
    return pl.pallas_call(
        _ada_kernel,
        grid=(depth, n // tn),
        in_specs=[
            pl.BlockSpec((r, d), lambda l, j: (0, 0)),
            pl.BlockSpec((None, d, tn), lambda l, j: (l, 0, j)),
            pl.BlockSpec((None, 1, tn), lambda l, j: (l, 0, j)),
        ],
        out_specs=pl.BlockSpec((None, r, tn), lambda l, j: (l, 0, j)),
        out_shape=jax.ShapeDtypeStruct((depth, r, n), F32),
        compiler_params=_params("parallel", "parallel"),
        name="ada",
    )(c, w_ada, b_ada.reshape(depth, 1, n))


def _modulate_rows(x_ref, g_ref, sc_ref, sh_ref, h_scr):
    @pl.when(pl.program_id(1) == 0)
    def _():
        h = _rms(x_ref[...]) * g_ref[...]
        h_scr[...] = (h * (1.0 + sc_ref[...]) + sh_ref[...]).astype(BF16)


def _project_tile(h_scr, w_ref, cg_ref, o_ref, head_norm):
    acc = _dot(h_scr[...], w_ref[...])
    if not head_norm:
        o_ref[...] = acc
        return
    for c in range(acc.shape[1] // DH_A):
        sl = slice(c * DH_A, (c + 1) * DH_A)
        o_ref[:, sl] = _rms(acc[:, sl]) * cg_ref[:, sl]


def _proj_kernel(x_ref, g_ref, sc_ref, sh_ref, w_ref, cg_ref, o_ref, h_scr, *, n_norm_tiles):
    j = pl.program_id(1)
    _modulate_rows(x_ref, g_ref, sc_ref, sh_ref, h_scr)

    @pl.when(j < n_norm_tiles)
    def _():
        _project_tile(h_scr, w_ref, cg_ref, o_ref, True)

    @pl.when(j >= n_norm_tiles)
    def _():
        _project_tile(h_scr, w_ref, cg_ref, o_ref, False)


def _kv_kernel(x_ref, g_ref, sc_ref, sh_ref, w_ref, cg_ref, k_ref, v_ref, h_scr, *, n_k_tiles):
    j = pl.program_id(1)
    _modulate_rows(x_ref, g_ref, sc_ref, sh_ref, h_scr)

    @pl.when(j < n_k_tiles)
    def _():
        _project_tile(h_scr, w_ref, cg_ref, k_ref, True)

    @pl.when(j >= n_k_tiles)
    def _():
        _project_tile(h_scr, w_ref, cg_ref, v_ref, False)


def _proj_call(kernel_fn, x, g, sc, sh, w, colgain, tm, rows_per_mod, out_specs, out_shape, name):
    n, d = x.shape
    r = sc.shape[1]
    tn = PROJ_TN
    mod_spec = pl.BlockSpec((None, r, d), lambda i, j: (i // rows_per_mod, 0, 0))
    return pl.pallas_call(
        kernel_fn,
        grid=(n // tm, w.shape[1] // tn),
        in_specs=[
            pl.BlockSpec((tm, d), lambda i, j: (i, 0)),
            pl.BlockSpec((1, d), lambda i, j: (0, 0)),
            mod_spec,
            mod_spec,
            pl.BlockSpec((d, tn), lambda i, j: (0, j)),
            pl.BlockSpec((1, tn), lambda i, j: (0, j)),
        ],
        out_specs=out_specs,
        out_shape=out_shape,
        scratch_shapes=[pltpu.VMEM((tm, d), BF16)],
        compiler_params=_params("parallel", "arbitrary"),
        name=name,
    )(x, g, sc, sh, w, colgain)


def _proj(x, g, sc, sh, w, colgain, tm, rows_per_mod):
    n = x.shape[0]
    tn = PROJ_TN
    return _proj_call(functools.partial(_proj_kernel, n_norm_tiles=COL_QB // tn), x, g, sc, sh, w, colgain,
                      tm, rows_per_mod, pl.BlockSpec((tm, tn), lambda i, j: (i, j)),
                      jax.ShapeDtypeStruct((n, N_COLS), F32), "proj")


def _proj_kv(x, g, sc, sh, w, colgain, tm, rows_per_mod):
    n, d = x.shape
    tn = PROJ_TN
    nk = d // tn
    out = jax.ShapeDtypeStruct((n, d), F32)
    return _proj_call(functools.partial(_kv_kernel, n_k_tiles=nk), x, g, sc, sh, w, colgain, tm, rows_per_mod,
                      [pl.BlockSpec((tm, tn), lambda i, j: (i, jnp.minimum(j, nk - 1))),
                       pl.BlockSpec((tm, tn), lambda i, j: (i, jnp.maximum(j - nk, 0)))],
                      [out, out], "proj_kv")


SB_HEADS_PER_STEP = 4


SB_TQ = 512
SB_TK = 256
LOG2E = 1.4426950408889634
SB_QSCALE = DH_A ** -0.5 * LOG2E


def _sb_softplus2(z):
    neg_abs = lax.bitcast_convert_type(lax.bitcast_convert_type(z, jnp.uint32) | jnp.uint32(0x80000000), F32)
    return jnp.maximum(z, 0.0) + jnp.log(1.0 + jnp.exp2(neg_abs)) * LOG2E


def _sb_weights(z, sp, drop, u, run):
    return jnp.exp2(z - sp - _dot(drop.astype(BF16), u) - run)


def _sb_prompt_kernel(bias_ref, q_ref, k_ref, v_ref, u_ref, o_ref, run_scr, *, tq, tk, nh):
    hg = pl.program_id(1)
    qi = pl.program_id(2)
    u = u_ref[...]
    heads = [slice(i * DH_A, (i + 1) * DH_A) for i in range(nh)]
    qs = [(q_ref[:, hs] * SB_QSCALE).astype(BF16) for hs in heads]
    bias = [bias_ref[hg * nh + i] * LOG2E for i in range(nh)]
    row = lax.broadcasted_iota(jnp.int32, (tq, tk), 0)
    col = lax.broadcasted_iota(jnp.int32, (tq, tk), 1)
    o_ref[...] = jnp.zeros_like(o_ref)
    run_scr[...] = jnp.zeros_like(run_scr)

    def block(off, visible):
        for i, hs in enumerate(heads):
            k = k_ref[pl.ds(off, tk), hs].astype(BF16)
            v = v_ref[pl.ds(off, tk), hs].astype(BF16)
            z = _dot_nt(qs[i], k) + bias[i]
            sp = _sb_softplus2(z)
            drop = sp if visible is None else jnp.where(visible, sp, 0.0)
            w = _sb_weights(z, sp, drop, u, run_scr[i])
            if visible is not None:
                w = jnp.where(visible, w, 0.0)
            o_ref[:, hs] += _dot(w.astype(BF16), v)
            run_scr[i] += jnp.sum(drop, axis=1, keepdims=True)

    for d in reversed(range(tq // tk)):
        block(pl.multiple_of(qi * tq + d * tk, tk), col + d * tk < row)

    def body(jj, carry):
        block(pl.multiple_of(qi * tq - (jj + 1) * tk, tk), None)
        return carry

    lax.fori_loop(0, qi * (tq // tk), body, 0)


def _strict_lower(n):
    r = np.arange(n)
    return jnp.asarray(r[:, None] > r[None, :], dtype=BF16)


def _sb_prompt(proj, k, v, bias, batch, seq):
    tq, tk = min(SB_TQ, seq), SB_TK
    nq = seq // tq
    nh = SB_HEADS_PER_STEP
    w = nh * DH_A
    return pl.pallas_call(
        functools.partial(_sb_prompt_kernel, tq=tq, tk=tk, nh=nh),
        grid_spec=pltpu.PrefetchScalarGridSpec(
            num_scalar_prefetch=1,
            grid=(batch, H_A // nh, nq),
            in_specs=[
                pl.BlockSpec((tq, w), lambda b, h, i, s: (b * nq + i, COL_QA // w + h)),
                pl.BlockSpec((seq, w), lambda b, h, i, s: (b, h)),
                pl.BlockSpec((seq, w), lambda b, h, i, s: (b, h)),
                pl.BlockSpec((tk, tk), lambda b, h, i, s: (0, 0)),
            ],
            out_specs=pl.BlockSpec((tq, w), lambda b, h, i, s: (b * nq + i, h)),
            scratch_shapes=[pltpu.VMEM((nh, tq, 1), F32)],
        ),
        out_shape=jax.ShapeDtypeStruct((batch * seq, D_MODEL), F32),
        compiler_params=_params("parallel", "parallel", "arbitrary"),
        name="sb_prompt",
    )(bias, proj, k, v, _strict_lower(tk))


def _sb_sample_kernel(pt_ref, q_ref, kn_ref, vn_ref, *rest, n_pages, page):
    kc_refs, vc_refs = rest[:n_pages], rest[n_pages:2 * n_pages]
    bias_ref, u_ref, o_ref, kpad, vpad = rest[2 * n_pages:]
    u = u_ref[...]
    bias = bias_ref[...]
    heads = [slice(h * DH_A, (h + 1) * DH_A) for h in range(H_A)]
    qs = [(q_ref[:, hs] * SB_QSCALE).astype(BF16) for hs in heads]
    rows = H_A * SUBLANE

    def block(k_of, v_of, before, run, acc):
        z = jnp.concatenate([_dot_nt(q, k_of(h)) for h, q in enumerate(qs)], axis=0) + bias
        sp = _sb_softplus2(z)
        drop = sp if before is None else jnp.where(before, sp, 0.0)
        w = _sb_weights(z, sp, drop, u, run)
        if before is not None:
            w = jnp.where(before, w, 0.0)
        acc = [acc[h] + _dot(w[h * SUBLANE:(h + 1) * SUBLANE].astype(BF16), v_of(h)) for h in range(H_A)]
        return run + jnp.sum(drop, axis=1, keepdims=True), acc

    kpad[...] = jnp.zeros_like(kpad)
    vpad[...] = jnp.zeros_like(vpad)
    kpad[0:SUBLANE, :] = kn_ref[...]
    vpad[0:SUBLANE, :] = vn_ref[...]
    t_of_row = lax.broadcasted_iota(jnp.int32, (rows, page), 0) & (SUBLANE - 1)
    before = lax.broadcasted_iota(jnp.int32, (rows, page), 1) < t_of_row
    run = jnp.zeros((rows, 1), F32)
    acc = [jnp.zeros((SUBLANE, DH_A), F32) for _ in range(H_A)]
    run, acc = block(lambda h: kpad[:, heads[h]].astype(BF16), lambda h: vpad[:, heads[h]].astype(BF16),
                     before, run, acc)
    for p in reversed(range(n_pages)):
        kc, vc = kc_refs[p], vc_refs[p]
        run, acc = block(lambda h: kc[pl.ds(h, page, stride=H_A), :].astype(BF16),
                         lambda h: vc[pl.ds(h, page, stride=H_A), :].astype(BF16), None, run, acc)
    o_ref[...] = jnp.concatenate(acc, axis=1)[0:o_ref.shape[0]]


def _sb_sample(layer, q, k_new, v_new, cache_k, cache_v, page_table, bias):
    nb, t_new, d = q.shape
    n_pages = page_table.shape[1]
    page = cache_k.shape[2] // H_A
    assert page == LANE and t_new <= SUBLANE
    pad_rows = ((0, 0), (0, SUBLANE - t_new), (0, 0))
    rows = H_A * SUBLANE
    bias_t = jnp.broadcast_to(jnp.repeat(bias * LOG2E, SUBLANE)[:, None], (rows, page))
    row_spec = pl.BlockSpec((None, SUBLANE, d), lambda b, pt: (b, 0, 0))

    def page_spec(p):
        return pl.BlockSpec((None, None, page * H_A, DH_A), lambda b, pt: (layer, pt[b, p], 0, 0))

    pages = [page_spec(p) for p in range(n_pages)]
    return pl.pallas_call(
        functools.partial(_sb_sample_kernel, n_pages=n_pages, page=page),
        grid_spec=pltpu.PrefetchScalarGridSpec(
            num_scalar_prefetch=1,
            grid=(nb,),
            in_specs=[row_spec, row_spec, row_spec] + pages + pages + [
                pl.BlockSpec((rows, page), lambda b, pt: (0, 0)),
                pl.BlockSpec((page, page), lambda b, pt: (0, 0)),
            ],
            out_specs=pl.BlockSpec((None, t_new, d), lambda b, pt: (b, 0, 0)),
            scratch_shapes=[pltpu.VMEM((page, d), F32), pltpu.VMEM((page, d), F32)],
        ),
        out_shape=jax.ShapeDtypeStruct((nb, t_new, d), F32),
        compiler_params=_params("parallel"),
        name="sb_sample",
    )(page_table, jnp.pad(q, pad_rows), jnp.pad(k_new, pad_rows), jnp.pad(v_new, pad_rows),
      *([cache_k] * n_pages), *([cache_v] * n_pages), bias_t, _strict_lower(page))


def _gla_kernel(q_ref, k_ref, v_ref, al_ref, wg_ref, bg_ref, on_ref, s0_ref, tri_ref, o_ref, sfin_ref,
                s_scr, *, chunk, t_valid):
    c = pl.program_id(1)

    @pl.when(c == 0)
    def _():
        s_scr[...] = s0_ref[...]

    tri = tri_ref[...]
    lower = lax.broadcasted_iota(jnp.int32, (chunk, chunk), 0) >= lax.broadcasted_iota(jnp.int32, (chunk, chunk), 1)
    ones = jnp.ones((chunk, DK_B), BF16)
    for b in range(s_scr.shape[0]):
        gate = _dot(al_ref[b].astype(BF16), wg_ref[...]) + bg_ref[...]
        log_a = -_softplus(-gate) * (1.0 / GATE_NORM)
        if t_valid < chunk:
            rows = lax.broadcasted_iota(jnp.int32, log_a.shape, 0)
            log_a = jnp.where(rows < t_valid, log_a, 0.0)
        for h in range(H_B):
            ks = slice(h * DK_B, (h + 1) * DK_B)
            vs = slice(h * DV_B, (h + 1) * DV_B)
            hi, lo = _split_bf16(log_a[:, ks])
            cum = _dot(tri, hi) + _dot(tri, lo)
            last = cum[chunk - 1:chunk, :]
            last_col = jnp.exp(_dot_tn(hi, ones) + _dot_tn(lo, ones))
            q = q_ref[b, :, ks] * (DK_B ** -0.5)
            k = k_ref[b, :, ks]
            v = v_ref[b, :, vs].astype(BF16)
            q_dec = (q * jnp.exp(cum)).astype(BF16)
            k_dec = (k * jnp.exp(-cum)).astype(BF16)
            att = jnp.where(lower, _dot_nt(q_dec, k_dec), 0.0)
            s = s_scr[b, h]
            o = _dot(q_dec, s.astype(BF16)) + _dot(att.astype(BF16), v)
            k_rem = (k * jnp.exp(last - cum)).astype(BF16)
            s_scr[b, h] = jnp.concatenate([last_col] * (DV_B // DK_B), axis=1) * s + _dot_tn(k_rem, v)
            o_ref[b, :, vs] = _rms(o) * on_ref[...]

    @pl.when(c == pl.num_programs(1) - 1)
    def _():
        sfin_ref[...] = s_scr[...]


GLA_SEQS_PER_STEP = 4


def _gla(proj, w_gk2, b_gk2, onorm_g, s0, nb, seq, chunk, t_valid):
    nc = seq // chunk
    ns = GLA_SEQS_PER_STEP
    assert nb % ns == 0
    wg = jnp.zeros((LANE, H_B * DK_B), F32).at[:GATE_RANK].set(w_gk2).astype(BF16)
    r = np.arange(chunk)
    tri = jnp.asarray(r[:, None] >= r[None, :], dtype=BF16)
    kw = H_B * DK_B
    vw = H_B * DV_B
    p3 = proj.reshape(nb, seq, N_COLS)
    state_spec = pl.BlockSpec((ns, H_B, DK_B, DV_B), lambda b, c: (b, 0, 0, 0))
    o, s_fin = pl.pallas_call(
        functools.partial(_gla_kernel, chunk=chunk, t_valid=t_valid),
        grid=(nb // ns, nc),
        in_specs=[
            pl.BlockSpec((ns, chunk, kw), lambda b, c: (b, c, COL_QB // kw)),
            pl.BlockSpec((ns, chunk, kw), lambda b, c: (b, c, COL_KB // kw)),
            pl.BlockSpec((ns, chunk, vw), lambda b, c: (b, c, COL_VB // vw)),
            pl.BlockSpec((ns, chunk, LANE), lambda b, c: (b, c, COL_AL // LANE)),
            pl.BlockSpec((LANE, kw), lambda b, c: (0, 0)),
            pl.BlockSpec((1, kw), lambda b, c: (0, 0)),
            pl.BlockSpec((1, DV_B), lambda b, c: (0, 0)),
            state_spec,
            pl.BlockSpec((chunk, chunk), lambda b, c: (0, 0)),
        ],
        out_specs=[
            pl.BlockSpec((ns, chunk, vw), lambda b, c: (b, c, 0)),
            state_spec,
        ],
        out_shape=[
            jax.ShapeDtypeStruct((nb, seq, vw), F32),
            jax.ShapeDtypeStruct((nb, H_B, DK_B, DV_B), F32),
        ],
        scratch_shapes=[pltpu.VMEM((ns, H_B, DK_B, DV_B), F32)],
        compiler_params=_params("parallel", "arbitrary"),
        name="gla",
    )(p3, p3, p3, p3, wg, b_gk2.reshape(1, kw), onorm_g.reshape(1, DV_B), s0, tri)
    return o.reshape(nb * seq, vw), s_fin


def _mix_kernel(ga_ref, gb_ref, oa_ref, ob_ref, w_ref, x_ref, gt_ref, o_ref):
    def sig(t):
        return 1.0 / (1.0 + jnp.exp(-t))

    merged = sig(ga_ref[...]) * oa_ref[...] + sig(gb_ref[...]) * ob_ref[...]
    o_ref[...] = x_ref[...] + gt_ref[...] * _dot(merged.astype(BF16), w_ref[...])


def _mix_out(proj, o_a, o_b, w_out, x, gt, tm, rows_per_mod):
    n, d = x.shape
    r = gt.shape[1]
    row = lambda i: (i, 0)
    return pl.pallas_call(
        _mix_kernel,
        grid=(n // tm,),
        in_specs=[
            pl.BlockSpec((tm, d), lambda i: (i, COL_GA // d)),
            pl.BlockSpec((tm, d), lambda i: (i, COL_GB // d)),
            pl.BlockSpec((tm, d), row),
            pl.BlockSpec((tm, d), row),
            pl.BlockSpec((d, d), lambda i: (0, 0)),
            pl.BlockSpec((tm, d), row),
            pl.BlockSpec((None, r, d), lambda i: (i // rows_per_mod, 0, 0)),
        ],
        out_specs=pl.BlockSpec((tm, d), row),
        out_shape=jax.ShapeDtypeStruct((n, d), F32),
        compiler_params=_params("parallel"),
        name="mix_out",
    )(proj, proj, o_a, o_b, w_out, x, gt)


def _router_kernel(x_ref, g_ref, sc_ref, sh_ref, whi_ref, wlo_ref, b_ref, h_ref, wt_ref):
    h = _rms(x_ref[...]) * g_ref[...]
    h = h * (1.0 + sc_ref[...]) + sh_ref[...]
    h_ref[...] = h.astype(BF16)
    hi, lo = _split_bf16(h)
    lg = _dot(hi, whi_ref[...]) + (_dot(lo, whi_ref[...]) + _dot(hi, wlo_ref[...])) + b_ref[...]

    lane = lax.broadcasted_iota(jnp.int32, lg.shape, 1).astype(F32)
    past_end = float(ROUTER_PAD)

    def first_max(vals):
        m = jnp.max(vals, axis=1, keepdims=True)
        return m, jnp.min(jnp.where(vals == m, lane, past_end), axis=1, keepdims=True)

    is_group = lane < N_GROUPS
    g_max, g_idx = first_max(jnp.where(is_group, lg, -jnp.inf))
    g_w = 1.0 / jnp.sum(jnp.where(is_group, jnp.exp(lg - g_max), 0.0), axis=1, keepdims=True)
    first = N_GROUPS + g_idx * EXPERTS_PER_GROUP
    cand = jnp.where((lane >= first) & (lane < first + EXPERTS_PER_GROUP), lg, -jnp.inf)
    v1, i1 = first_max(cand)
    v2, i2 = first_max(jnp.where(lane == i1, -jnp.inf, cand))
    t = jnp.exp(v2 - v1)
    p1 = 1.0 / (1.0 + t)
    wt_ref[...] = jnp.where(lane == i1, p1 * g_w, 0.0) + jnp.where(lane == i2, t * p1 * g_w, 0.0)


def _router(x, g, sc, sh, w_r, b_r, tm, rows_per_mod):
    n, d = x.shape
    r = sc.shape[1]
    whi, wlo = _split_bf16(w_r)
    mod_spec = pl.BlockSpec((None, r, d), lambda i: (i // rows_per_mod, 0, 0))
    full = lambda i: (0, 0)
    return pl.pallas_call(
        _router_kernel,
        grid=(n // tm,),
        in_specs=[
            pl.BlockSpec((tm, d), lambda i: (i, 0)),
            pl.BlockSpec((1, d), full),
            mod_spec,
            mod_spec,
            pl.BlockSpec((d, ROUTER_PAD), full),
            pl.BlockSpec((d, ROUTER_PAD), full),
            pl.BlockSpec((1, ROUTER_PAD), full),
        ],
        out_specs=[
            pl.BlockSpec((tm, d), lambda i: (i, 0)),
            pl.BlockSpec((tm, ROUTER_PAD), lambda i: (i, 0)),
        ],
        out_shape=[
            jax.ShapeDtypeStruct((n, d), BF16),
            jax.ShapeDtypeStruct((n, ROUTER_PAD), F32),
        ],
        compiler_params=_params("parallel"),
        name="router",
    )(x, g, sc, sh, whi, wlo, b_r)


def _moe_kernel(h_ref, wt_ref, wg_ref, wu_ref, wd_ref, x_ref, gt_ref, o_ref, acc_scr):
    e = pl.program_id(1)

    @pl.when(e == 0)
    def _():
        acc_scr[...] = jnp.zeros_like(acc_scr)

    h = h_ref[...]
    lane = lax.broadcasted_iota(jnp.int32, wt_ref.shape, 1)
    we = jnp.sum(jnp.where(lane == e + N_GROUPS, wt_ref[...], 0.0), axis=1, keepdims=True)
    a = _dot(h, wg_ref[...])
    hid = a * (1.0 / (1.0 + jnp.exp(-a))) * _dot(h, wu_ref[...])
    hid = jnp.where(we != 0.0, hid * we, 0.0)
    acc_scr[...] += _dot(hid.astype(BF16), wd_ref[...])

    @pl.when(e == pl.num_programs(1) - 1)
    def _():
        o_ref[...] = x_ref[...] + gt_ref[...] * acc_scr[...]


def _moe(h, wt, wg, wu, wd, x, gt, tm, rows_per_mod):
    n, d = x.shape
    r = gt.shape[1]
    row = lambda i, e: (i, 0)
    return pl.pallas_call(
        _moe_kernel,
        grid=(n // tm, N_EXPERTS),
        in_specs=[
            pl.BlockSpec((tm, d), row),
            pl.BlockSpec((tm, ROUTER_PAD), row),
            pl.BlockSpec((None, d, D_EXPERT), lambda i, e: (e, 0, 0)),
            pl.BlockSpec((None, d, D_EXPERT), lambda i, e: (e, 0, 0)),
            pl.BlockSpec((None, D_EXPERT, d), lambda i, e: (e, 0, 0)),
            pl.BlockSpec((tm, d), row),
            pl.BlockSpec((None, r, d), lambda i, e: (i // rows_per_mod, 0, 0)),
        ],
        out_specs=pl.BlockSpec((tm, d), row),
        out_shape=jax.ShapeDtypeStruct((n, d), F32),
        scratch_shapes=[pltpu.VMEM((tm, d), F32)],
        compiler_params=_params("parallel", "arbitrary"),
        name="moe",
    )(h, wt, wg, wu, wd, x, gt)


def _layer_weights(l, w_in, qn_g, kn_g, w_out, w_rg, b_rg, w_re, b_re, w_e_gate, w_e_up, w_e_down):
    d = w_in.shape[1]
    offs = np.cumsum([0, H_A * DH_A, H_A * DH_A, H_A * DH_A, H_B * DK_B, H_B * DK_B, H_B * DV_B,
                      GATE_RANK, D_MODEL, D_MODEL])
    wl = w_in[l]
    seg = [wl[:, offs[i]:offs[i + 1]] for i in range(9)]
    qa, ka, va, qb, kb, vb, al, ga, gb = seg
    al = jnp.pad(al, ((0, 0), (0, AL_PAD - GATE_RANK)))
    w_proj = jnp.concatenate([qa, qb, kb, vb, ga, gb, al], axis=1).astype(BF16)
    colgain = jnp.concatenate([jnp.tile(qn_g[l], H_A), jnp.ones((N_COLS - COL_QB,), F32)]).reshape(1, N_COLS)
    w_kv = jnp.concatenate([ka, va], axis=1).astype(BF16)
    kv_gain = jnp.concatenate([jnp.tile(kn_g[l], H_A), jnp.ones((D_MODEL,), F32)]).reshape(1, 2 * D_MODEL)
    pad = ROUTER_PAD - N_GROUPS - N_EXPERTS
    w_r = jnp.pad(jnp.concatenate([w_rg[l], w_re[l]], axis=1), ((0, 0), (0, pad)))
    b_r = jnp.pad(jnp.concatenate([b_rg[l], b_re[l]]), (0, pad)).reshape(1, ROUTER_PAD)
    return dict(w_proj=w_proj, colgain=colgain, w_kv=w_kv, kv_gain=kv_gain,
                w_out=w_out[l].astype(BF16), w_r=w_r, b_r=b_r,
                wg=w_e_gate[l].astype(BF16), wu=w_e_up[l].astype(BF16), wd=w_e_down[l].astype(BF16))


def _trunk_layer(x, mod, lw, sb_fn, gla_fn, norm1_g, norm2_g, tm, rows_per_mod):
    sh1, sc1, gt1, sh2, sc2, gt2 = mod
    up = PROJ_ROW_SCALE if rows_per_mod % PROJ_ROW_SCALE == 0 else 1
    proj = _proj(x, norm1_g, sc1, sh1, lw["w_proj"], lw["colgain"], tm * up, rows_per_mod // up)
    k, v = _proj_kv(x, norm1_g, sc1, sh1, lw["w_kv"], lw["kv_gain"], tm * up, rows_per_mod // up)
    o_a = sb_fn(proj, k, v)
    o_b, s_new = gla_fn(proj)
    tm_mix = min(tm, 512)
    x = _mix_out(proj, o_a, o_b, lw["w_out"], x, gt1, tm_mix, rows_per_mod * (tm // tm_mix))
    h, wt = _router(x, norm2_g, sc2, sh2, lw["w_r"], lw["b_r"], tm, rows_per_mod)
    x = _moe(h, wt, lw["wg"], lw["wu"], lw["wd"], x, gt2, tm, rows_per_mod)
    return x, k, v, s_new


def kernel(x_prompt, x_sample, cache_k, cache_v, state_gla, page_table, c_prompt, c_sample,
           norm1_g, norm2_g, w_ada, b_ada, w_in, qn_g, kn_g, sb_bias, w_gk2, b_gk2, onorm_g, w_out,
           w_rg, b_rg, w_re, b_re, w_e_gate, w_e_up, w_e_down):
    depth = w_in.shape[0]
    bp, seq, d = x_prompt.shape
    bs, t_s, _ = x_sample.shape
    n_pool, page = cache_k.shape[1], cache_k.shape[2]
    n_p, n_s = bp * seq, bs * t_s
    tm_p = min(1024, seq)
    assert seq % tm_p == 0 and n_s % SUBLANE == 0

    c_all = jnp.concatenate([c_prompt, c_sample], axis=0)
    r_pad = -c_all.shape[0] % SUBLANE
    mod_all = _ada(jnp.pad(c_all, ((0, r_pad), (0, 0))), w_ada, b_ada)

    ck = cache_k.reshape(depth, n_pool, page * H_A, DH_A)
    cv = cache_v.reshape(depth, n_pool, page * H_A, DH_A)
    gla_zero = jnp.zeros((bp, H_B, DK_B, DV_B), F32)

    yp = x_prompt.reshape(n_p, d)
    ys = x_sample.reshape(n_s, d)
    outs = [[] for _ in range(6)]
    for l in range(depth):
        lw = _layer_weights(l, w_in, qn_g, kn_g, w_out, w_rg, b_rg, w_re, b_re, w_e_gate, w_e_up, w_e_down)
        n1, n2 = norm1_g[l].reshape(1, d), norm2_g[l].reshape(1, d)
        mod_p = [m.reshape(bp, 1, d) for m in jnp.split(mod_all[l, :bp], 6, axis=-1)]
        mod_s = [jnp.repeat(m, t_s, axis=0).reshape(1, n_s, d)
                 for m in jnp.split(mod_all[l, bp:bp + bs], 6, axis=-1)]

        sb_p = functools.partial(_sb_prompt, bias=sb_bias[l], batch=bp, seq=seq)
        gla_p = functools.partial(_gla, w_gk2=w_gk2[l], b_gk2=b_gk2[l], onorm_g=onorm_g[l], s0=gla_zero,
                                  nb=bp, seq=seq, chunk=min(GLA_CHUNK, seq), t_valid=min(GLA_CHUNK, seq))
        yp, k_p, v_p, s_p = _trunk_layer(yp, mod_p, lw, sb_p, gla_p, n1, n2, tm_p, seq // tm_p)
        outs[0].append(k_p.reshape(bp, seq, H_A, DH_A))
        outs[1].append(v_p.reshape(bp, seq, H_A, DH_A))
        outs[2].append(s_p)

        def sb_s(proj, k, v, l=l):
            q = proj[:, COL_QA:COL_QB].reshape(bs, t_s, d)
            o = _sb_sample(l, q, k.reshape(bs, t_s, d), v.reshape(bs, t_s, d), ck, cv, page_table, sb_bias[l])
            return o.reshape(n_s, d)

        def gla_s(proj, l=l):
            p3 = jnp.pad(proj.reshape(bs, t_s, N_COLS), ((0, 0), (0, SUBLANE - t_s), (0, 0)))
            o, s_new = _gla(p3.reshape(bs * SUBLANE, N_COLS), w_gk2[l], b_gk2[l], onorm_g[l], state_gla[l],
                            nb=bs, seq=SUBLANE, chunk=SUBLANE, t_valid=t_s)
            return o.reshape(bs, SUBLANE, d)[:, :t_s].reshape(n_s, d), s_new

        ys, k_s, v_s, s_s = _trunk_layer(ys, mod_s, lw, sb_s, gla_s, n1, n2, n_s, 1)
        outs[3].append(k_s.reshape(bs, t_s, H_A, DH_A))
        outs[4].append(v_s.reshape(bs, t_s, H_A, DH_A))
        outs[5].append(s_s.astype(state_gla.dtype))

    return (yp.reshape(bp, seq, d), ys.reshape(bs, t_s, d), jnp.stack(outs[0]), jnp.stack(outs[1]),
            jnp.stack(outs[2]).astype(state_gla.dtype), jnp.stack(outs[3]), jnp.stack(outs[4]), jnp.stack(outs[5]))
```

```python
import functools

import jax
import jax.numpy as jnp
import numpy as np
from jax import lax
from jax.experimental import pallas as pl
from jax.experimental.pallas import tpu as pltpu

F32 = jnp.float32
BF16 = jnp.bfloat16

LANE = 128
SUBLANE = 8
VMEM_LIMIT = 56 * 1024 * 1024

D_MODEL = 1024
DH_A = 128
H_A = D_MODEL // DH_A
H_B = 4
DK_B = D_MODEL // (2 * H_B)
DV_B = D_MODEL // H_B
GATE_RANK = 16
GATE_NORM = 16.0
GLA_CHUNK = 128
N_GROUPS = 4
EXPERTS_PER_GROUP = 8
N_EXPERTS = N_GROUPS * EXPERTS_PER_GROUP
TOP_K = 2
D_EXPERT = D_MODEL // 4
EPS = 1e-6

COL_QA = 0
COL_QB = COL_QA + H_A * DH_A
COL_KB = COL_QB + H_B * DK_B
COL_VB = COL_KB + H_B * DK_B
COL_GA = COL_VB + H_B * DV_B
COL_GB = COL_GA + D_MODEL
COL_AL = COL_GB + D_MODEL
AL_PAD = 512
N_COLS = COL_AL + AL_PAD
PROJ_TN = 512
PROJ_ROW_SCALE = 2
ROUTER_PAD = LANE


def _params(*sem):
    return pltpu.CompilerParams(dimension_semantics=sem, vmem_limit_bytes=VMEM_LIMIT)


def _softplus(z):
    return jnp.maximum(z, 0.0) + jnp.log1p(jnp.exp(-jnp.abs(z)))


def _split_bf16(x):
    hi = x.astype(BF16)
    lo = (x - hi.astype(F32)).astype(BF16)
    return hi, lo


def _dot(a, b):
    return jnp.dot(a, b, preferred_element_type=F32)


def _dot_nt(a, b):
    return lax.dot_general(a, b, (((1,), (1,)), ((), ())), preferred_element_type=F32)


def _dot_tn(a, b):
    return lax.dot_general(a, b, (((0,), (0,)), ((), ())), preferred_element_type=F32)


def _rms(x):
    return x * lax.rsqrt(jnp.mean(x * x, axis=-1, keepdims=True) + EPS)


def _ada_kernel(c_ref, w_ref, b_ref, o_ref):
    c = c_ref[...]
    a = (c * (1.0 / (1.0 + jnp.exp(-c)))).astype(BF16)
    o_ref[...] = _dot(a, w_ref[...].astype(BF16)) + b_ref[...]


def _ada(c, w_ada, b_ada):
    depth, d, n = w_ada.shape
    r = c.shape[0]
    tn = 1024
    return pl.pallas_call(
        _ada_kernel,
        grid=(depth, n // tn),
        in_specs=[
            pl.BlockSpec((r, d), lambda l, j: (0, 0)),
            pl.BlockSpec((None, d, tn), lambda l, j: (l, 0, j)),
            pl.BlockSpec((None, 1, tn), lambda l, j: (l, 0, j)),
        ],
        out_specs=pl.BlockSpec((None, r, tn), lambda l, j: (l, 0, j)),
        out_shape=jax.ShapeDtypeStruct((depth, r, n), F32),
        compiler_params=_params("parallel", "parallel"),
        name="ada",
    )(c, w_ada, b_ada.reshape(depth, 1, n))


def _modulate_rows(x_ref, g_ref, sc_ref, sh_ref, h_scr):
    @pl.when(pl.program_id(1) == 0)
    def _():
        h = _rms(x_ref[...]) * g_ref[...]
        h_scr[...] = (h * (1.0 + sc_ref[...]) + sh_ref[...]).astype(BF16)


def _project_tile(h_scr, w_ref, cg_ref, o_ref, head_norm):
    acc = _dot(h_scr[...], w_ref[...])
    if not head_norm:
        o_ref[...] = acc
        return
    for c in range(acc.shape[1] // DH_A):
        sl = slice(c * DH_A, (c + 1) * DH_A)
        o_ref[:, sl] = _rms(acc[:, sl]) * cg_ref[:, sl]


def _proj_kernel(x_ref, g_ref, sc_ref, sh_ref, w_ref, cg_ref, o_ref, h_scr, *, n_norm_tiles):
    j = pl.program_id(1)
    _modulate_rows(x_ref, g_ref, sc_ref, sh_ref, h_scr)

    @pl.when(j < n_norm_tiles)
    def _():
        _project_tile(h_scr, w_ref, cg_ref, o_ref, True)

    @pl.when(j >= n_norm_tiles)
    def _():
        _project_tile(h_scr, w_ref, cg_ref, o_ref, False)


def _kv_kernel(x_ref, g_ref, sc_ref, sh_ref, w_ref, cg_ref, k_ref, v_ref, h_scr, *, n_k_tiles):
    j = pl.program_id(1)
    _modulate_rows(x_ref, g_ref, sc_ref, sh_ref, h_scr)

    @pl.when(j < n_k_tiles)
    def _():
        _project_tile(h_scr, w_ref, cg_ref, k_ref, True)

    @pl.when(j >= n_k_tiles)
    def _():
        _project_tile(h_scr, w_ref, cg_ref, v_ref, False)


def _proj_call(kernel_fn, x, g, sc, sh, w, colgain, tm, rows_per_mod, out_specs, out_shape, name):
    n, d = x.shape
    r = sc.shape[1]
    tn = PROJ_TN
    mod_spec = pl.BlockSpec((None, r, d), lambda i, j: (i // rows_per_mod, 0, 0))
    return pl.pallas_call(
        kernel_fn,
        grid=(n // tm, w.shape[1] // tn),
        in_specs=[
            pl.BlockSpec((tm, d), lambda i, j: (i, 0)),
            pl.BlockSpec((1, d), lambda i, j: (0, 0)),
            mod_spec,
            mod_spec,
            pl.BlockSpec((d, tn), lambda i, j: (0, j)),
            pl.BlockSpec((1, tn), lambda i, j: (0, j)),
        ],
        out_specs=out_specs,
        out_shape=out_shape,
        scratch_shapes=[pltpu.VMEM((tm, d), BF16)],
        compiler_params=_params("parallel", "arbitrary"),
        name=name,
    )(x, g, sc, sh, w, colgain)


def _proj(x, g, sc, sh, w, colgain, tm, rows_per_mod):
    n = x.shape[0]
    tn = PROJ_TN
    return _proj_call(functools.partial(_proj_kernel, n_norm_tiles=COL_QB // tn), x, g, sc, sh, w, colgain,
                      tm, rows_per_mod, pl.BlockSpec((tm, tn), lambda i, j: (i, j)),
                      jax.ShapeDtypeStruct((n, N_COLS), F32), "proj")


def _proj_kv(x, g, sc, sh, w, colgain, tm, rows_per_mod):
    n, d = x.shape
    tn = PROJ_TN
    nk = d // tn
    out = jax.ShapeDtypeStruct((n, d), F32)
    return _proj_call(functools.partial(_kv_kernel, n_k_tiles=nk), x, g, sc, sh, w, colgain, tm, rows_per_mod,
                      [pl.BlockSpec((tm, tn), lambda i, j: (i, jnp.minimum(j, nk - 1))),
                       pl.BlockSpec((tm, tn), lambda i, j: (i, jnp.maximum(j - nk, 0)))],
                      [out, out], "proj_kv")


SB_HEADS_PER_STEP = 4


SB_TQ = 512
SB_TK = 256
LOG2E = 1.4426950408889634
SB_QSCALE = DH_A ** -0.5 * LOG2E


def _sb_softplus2(z):
    neg_abs = lax.bitcast_convert_type(lax.bitcast_convert_type(z, jnp.uint32) | jnp.uint32(0x80000000), F32)
    return jnp.maximum(z, 0.0) + jnp.log(1.0 + jnp.exp2(neg_abs)) * LOG2E


def _sb_weights(z, sp, drop, u, run):
    t = z - sp - _dot(drop.astype(BF16), u)
    return jnp.exp2(t if run is None else t - run)


def _sb_prompt_kernel(bias_ref, q_ref, k_ref, v_ref, u_ref, o_ref, run_scr, *, tq, tk, nh):
    hg = pl.program_id(1)
    qi = pl.program_id(2)
    u = u_ref[...]
    heads = [slice(i * DH_A, (i + 1) * DH_A) for i in range(nh)]
    qs = [(q_ref[:, hs] * SB_QSCALE).astype(BF16) for hs in heads]
    bias = [bias_ref[hg * nh + i] * LOG2E for i in range(nh)]
    row = lax.broadcasted_iota(jnp.int32, (tq, tk), 0)
    col = lax.broadcasted_iota(jnp.int32, (tq, tk), 1)
    o_ref[...] = jnp.zeros_like(o_ref)
    run_scr[...] = jnp.zeros_like(run_scr)

    def block(off, visible):
        for i, hs in enumerate(heads):
            k = k_ref[pl.ds(off, tk), hs].astype(BF16)
            v = v_ref[pl.ds(off, tk), hs].astype(BF16)
            z = _dot_nt(qs[i], k) + bias[i]
            sp = _sb_softplus2(z)
            drop = sp if visible is None else jnp.where(visible, sp, 0.0)
            w = _sb_weights(z, sp, drop, u, run_scr[i])
            if visible is not None:
                w = jnp.where(visible, w, 0.0)
            o_ref[:, hs] += _dot(w.astype(BF16), v)
            run_scr[i] += jnp.sum(drop, axis=1, keepdims=True)

    for d in reversed(range(tq // tk)):
        block(pl.multiple_of(qi * tq + d * tk, tk), col + d * tk < row)

    def body(jj, carry):
        block(pl.multiple_of(qi * tq - (jj + 1) * tk, tk), None)
        return carry

    lax.fori_loop(0, qi * (tq // tk), body, 0)


def _strict_lower(n):
    r = np.arange(n)
    return jnp.asarray(r[:, None] > r[None, :], dtype=BF16)


def _sb_prompt(proj, k, v, bias, batch, seq):
    tq, tk = min(SB_TQ, seq), SB_TK
    nq = seq // tq
    nh = SB_HEADS_PER_STEP
    w = nh * DH_A
    return pl.pallas_call(
        functools.partial(_sb_prompt_kernel, tq=tq, tk=tk, nh=nh),
        grid_spec=pltpu.PrefetchScalarGridSpec(
            num_scalar_prefetch=1,
            grid=(batch, H_A // nh, nq),
            in_specs=[
                pl.BlockSpec((tq, w), lambda b, h, i, s: (b * nq + i, COL_QA // w + h)),
                pl.BlockSpec((seq, w), lambda b, h, i, s: (b, h)),
                pl.BlockSpec((seq, w), lambda b, h, i, s: (b, h)),
                pl.BlockSpec((tk, tk), lambda b, h, i, s: (0, 0)),
            ],
            out_specs=pl.BlockSpec((tq, w), lambda b, h, i, s: (b * nq + i, h)),
            scratch_shapes=[pltpu.VMEM((nh, tq, 1), F32)],
        ),
        out_shape=jax.ShapeDtypeStruct((batch * seq, D_MODEL), F32),
        compiler_params=_params("parallel", "parallel", "arbitrary"),
        name="sb_prompt",
    )(bias, proj, k, v, _strict_lower(tk))


def _sb_sample_kernel(pt_ref, q_ref, kn_ref, vn_ref, *rest, n_pages, page):
    kc_refs, vc_refs = rest[:n_pages], rest[n_pages:2 * n_pages]
    bias_ref, u_ref, o_ref, kpad, vpad = rest[2 * n_pages:]
    heads = [slice(h * DH_A, (h + 1) * DH_A) for h in range(H_A)]
    qs = [(q_ref[:, hs] * SB_QSCALE).astype(BF16) for hs in heads]
    rows = H_A * SUBLANE

    def block(k_of, v_of, width, before):
        z = jnp.concatenate([_dot_nt(q, k_of(h)) for h, q in enumerate(qs)], axis=0) + bias_ref[:, 0:width]
        sp = _sb_softplus2(z)
        drop = sp if before is None else jnp.where(before, sp, 0.0)
        w = _sb_weights(z, sp, drop, u_ref[0:width, 0:width], None)
        if before is not None:
            w = jnp.where(before, w, 0.0)
        pv = [_dot(w[h * SUBLANE:(h + 1) * SUBLANE].astype(BF16), v_of(h)) for h in range(H_A)]
        return pv, jnp.sum(drop, axis=1, keepdims=True)

    def cached(refs, ps):
        return lambda h: jnp.concatenate([refs[p][pl.ds(h, page, stride=H_A), :] for p in ps], axis=0).astype(BF16)

    kpad[...] = jnp.zeros_like(kpad)
    vpad[...] = jnp.zeros_like(vpad)
    kpad[0:SUBLANE, :] = kn_ref[...]
    vpad[0:SUBLANE, :] = vn_ref[...]
    t_of_row = lax.broadcasted_iota(jnp.int32, (rows, page), 0) & (SUBLANE - 1)
    before = lax.broadcasted_iota(jnp.int32, (rows, page), 1) < t_of_row
    blocks = [block(lambda h: kpad[:, heads[h]].astype(BF16), lambda h: vpad[:, heads[h]].astype(BF16),
                    page, before)]
    for hi_page in range(n_pages - 1, -1, -2):
        ps = [p for p in (hi_page - 1, hi_page) if p >= 0]
        blocks.append(block(cached(kc_refs, ps), cached(vc_refs, ps), len(ps) * page, None))
    run = jnp.zeros((rows, 1), F32)
    acc = [jnp.zeros((SUBLANE, DH_A), F32) for _ in range(H_A)]
    for pv, row_drop in blocks:
        scale = jnp.exp2(-run)
        acc = [acc[h] + scale[h * SUBLANE:(h + 1) * SUBLANE] * pv[h] for h in range(H_A)]
        run = run + row_drop
    o_ref[...] = jnp.concatenate(acc, axis=1)[0:o_ref.shape[0]]


def _sb_sample(layer, q, k_new, v_new, cache_k, cache_v, page_table, bias):
    nb, t_new, d = q.shape
    n_pages = page_table.shape[1]
    page = cache_k.shape[2] // H_A
    assert page == LANE and t_new <= SUBLANE
    pad_rows = ((0, 0), (0, SUBLANE - t_new), (0, 0))
    rows = H_A * SUBLANE
    wide = 2 * page
    bias_t = jnp.broadcast_to(jnp.repeat(bias * LOG2E, SUBLANE)[:, None], (rows, wide))
    row_spec = pl.BlockSpec((None, SUBLANE, d), lambda b, pt: (b, 0, 0))

    def page_spec(p):
        return pl.BlockSpec((None, None, page * H_A, DH_A), lambda b, pt: (layer, pt[b, p], 0, 0))

    pages = [page_spec(p) for p in range(n_pages)]
    return pl.pallas_call(
        functools.partial(_sb_sample_kernel, n_pages=n_pages, page=page),
        grid_spec=pltpu.PrefetchScalarGridSpec(
            num_scalar_prefetch=1,
            grid=(nb,),
            in_specs=[row_spec, row_spec, row_spec] + pages + pages + [
                pl.BlockSpec((rows, wide), lambda b, pt: (0, 0)),
                pl.BlockSpec((wide, wide), lambda b, pt: (0, 0)),
            ],
            out_specs=pl.BlockSpec((None, t_new, d), lambda b, pt: (b, 0, 0)),
            scratch_shapes=[pltpu.VMEM((page, d), F32), pltpu.VMEM((page, d), F32)],
        ),
        out_shape=jax.ShapeDtypeStruct((nb, t_new, d), F32),
        compiler_params=_params("parallel"),
        name="sb_sample",
    )(page_table, jnp.pad(q, pad_rows), jnp.pad(k_new, pad_rows), jnp.pad(v_new, pad_rows),
      *([cache_k] * n_pages), *([cache_v] * n_pages), bias_t, _strict_lower(wide))


def _gla_kernel(q_ref, k_ref, v_ref, al_ref, wg_ref, bg_ref, on_ref, s0_ref, tri_ref, o_ref, sfin_ref,
                s_scr, *, chunk, t_valid):
    c = pl.program_id(1)

    @pl.when(c == 0)
    def _():
        s_scr[...] = s0_ref[...]

    tri = tri_ref[...]
    lower = lax.broadcasted_iota(jnp.int32, (chunk, chunk), 0) >= lax.broadcasted_iota(jnp.int32, (chunk, chunk), 1)
    ones = jnp.ones((chunk, DK_B), BF16)
    for b in range(s_scr.shape[0]):
        gate = _dot(al_ref[b].astype(BF16), wg_ref[...]) + bg_ref[...]
        log_a = -_softplus(-gate) * (1.0 / GATE_NORM)
        if t_valid < chunk:
            rows = lax.broadcasted_iota(jnp.int32, log_a.shape, 0)
            log_a = jnp.where(rows < t_valid, log_a, 0.0)
        for h in range(H_B):
            ks = slice(h * DK_B, (h + 1) * DK_B)
            vs = slice(h * DV_B, (h + 1) * DV_B)
            hi, lo = _split_bf16(log_a[:, ks])
            cum = _dot(tri, hi) + _dot(tri, lo)
            last = cum[chunk - 1:chunk, :]
            last_col = jnp.exp(_dot_tn(hi, ones) + _dot_tn(lo, ones))
            q = q_ref[b, :, ks] * (DK_B ** -0.5)
            k = k_ref[b, :, ks]
            v = v_ref[b, :, vs].astype(BF16)
            q_dec = (q * jnp.exp(cum)).astype(BF16)
            mid = cum[chunk // 2:chunk // 2 + 1, :]
            q_mid = (q * jnp.exp(cum - mid)).astype(BF16)
            k_mid = (k * jnp.exp(mid - cum)).astype(BF16)
            att = jnp.where(lower, _dot_nt(q_mid, k_mid), 0.0)
            s = s_scr[b, h]
            o = _dot(q_dec, s.astype(BF16)) + _dot(att.astype(BF16), v)
            k_rem = (k * jnp.exp(last - cum)).astype(BF16)
            s_scr[b, h] = jnp.concatenate([last_col] * (DV_B // DK_B), axis=1) * s + _dot_tn(k_rem, v)
            o_ref[b, :, vs] = _rms(o) * on_ref[...]

    @pl.when(c == pl.num_programs(1) - 1)
    def _():
        sfin_ref[...] = s_scr[...]


GLA_SEQS_PER_STEP = 4


def _gla(proj, w_gk2, b_gk2, onorm_g, s0, s0_layer, nb, seq, chunk, t_valid):
    nc = seq // chunk
    ns = GLA_SEQS_PER_STEP
    assert nb % ns == 0
    wg = jnp.zeros((LANE, H_B * DK_B), F32).at[:GATE_RANK].set(w_gk2).astype(BF16)
    r = np.arange(chunk)
    tri = jnp.asarray(r[:, None] >= r[None, :], dtype=BF16)
    kw = H_B * DK_B
    vw = H_B * DV_B
    p3 = proj.reshape(nb, seq, N_COLS)
    state_spec = pl.BlockSpec((ns, H_B, DK_B, DV_B), lambda b, c: (b, 0, 0, 0))
    o, s_fin = pl.pallas_call(
        functools.partial(_gla_kernel, chunk=chunk, t_valid=t_valid),
        grid=(nb // ns, nc),
        in_specs=[
            pl.BlockSpec((ns, chunk, kw), lambda b, c: (b, c, COL_QB // kw)),
            pl.BlockSpec((ns, chunk, kw), lambda b, c: (b, c, COL_KB // kw)),
            pl.BlockSpec((ns, chunk, vw), lambda b, c: (b, c, COL_VB // vw)),
            pl.BlockSpec((ns, chunk, LANE), lambda b, c: (b, c, COL_AL // LANE)),
            pl.BlockSpec((LANE, kw), lambda b, c: (0, 0)),
            pl.BlockSpec((1, kw), lambda b, c: (0, 0)),
            pl.BlockSpec((1, DV_B), lambda b, c: (0, 0)),
            pl.BlockSpec((None, ns, H_B, DK_B, DV_B), lambda b, c: (s0_layer, b, 0, 0, 0)),
            pl.BlockSpec((chunk, chunk), lambda b, c: (0, 0)),
        ],
        out_specs=[
            pl.BlockSpec((ns, chunk, vw), lambda b, c: (b, c, 0)),
            state_spec,
        ],
        out_shape=[
            jax.ShapeDtypeStruct((nb, seq, vw), F32),
            jax.ShapeDtypeStruct((nb, H_B, DK_B, DV_B), F32),
        ],
        scratch_shapes=[pltpu.VMEM((ns, H_B, DK_B, DV_B), F32)],
        compiler_params=_params("parallel", "arbitrary"),
        name="gla",
    )(p3, p3, p3, p3, wg, b_gk2.reshape(1, kw), onorm_g.reshape(1, DV_B), s0, tri)
    return o.reshape(nb * seq, vw), s_fin


def _mix_kernel(ga_ref, gb_ref, oa_ref, ob_ref, w_ref, x_ref, gt_ref, o_ref):
    def sig(t):
        return 1.0 / (1.0 + jnp.exp(-t))

    merged = sig(ga_ref[...]) * oa_ref[...] + sig(gb_ref[...]) * ob_ref[...]
    o_ref[...] = x_ref[...] + gt_ref[...] * _dot(merged.astype(BF16), w_ref[...])


def _mix_out(proj, o_a, o_b, w_out, x, gt, tm, rows_per_mod):
    n, d = x.shape
    r = gt.shape[1]
    row = lambda i: (i, 0)
    return pl.pallas_call(
        _mix_kernel,
        grid=(n // tm,),
        in_specs=[
            pl.BlockSpec((tm, d), lambda i: (i, COL_GA // d)),
            pl.BlockSpec((tm, d), lambda i: (i, COL_GB // d)),
            pl.BlockSpec((tm, d), row),
            pl.BlockSpec((tm, d), row),
            pl.BlockSpec((d, d), lambda i: (0, 0)),
            pl.BlockSpec((tm, d), row),
            pl.BlockSpec((None, r, d), lambda i: (i // rows_per_mod, 0, 0)),
        ],
        out_specs=pl.BlockSpec((tm, d), row),
        out_shape=jax.ShapeDtypeStruct((n, d), F32),
        compiler_params=_params("parallel"),
        name="mix_out",
    )(proj, proj, o_a, o_b, w_out, x, gt)


def _router_kernel(x_ref, g_ref, sc_ref, sh_ref, whi_ref, wlo_ref, b_ref, h_ref, wt_ref):
    h = _rms(x_ref[...]) * g_ref[...]
    h = h * (1.0 + sc_ref[...]) + sh_ref[...]
    h_ref[...] = h.astype(BF16)
    hi, lo = _split_bf16(h)
    lg = _dot(hi, whi_ref[...]) + (_dot(lo, whi_ref[...]) + _dot(hi, wlo_ref[...])) + b_ref[...]

    lane = lax.broadcasted_iota(jnp.int32, lg.shape, 1).astype(F32)
    past_end = float(ROUTER_PAD)

    def first_max(vals):
        m = jnp.max(vals, axis=1, keepdims=True)
        return m, jnp.min(jnp.where(vals == m, lane, past_end), axis=1, keepdims=True)

    is_group = lane < N_GROUPS
    g_max, g_idx = first_max(jnp.where(is_group, lg, -jnp.inf))
    g_w = 1.0 / jnp.sum(jnp.where(is_group, jnp.exp(lg - g_max), 0.0), axis=1, keepdims=True)
    first = N_GROUPS + g_idx * EXPERTS_PER_GROUP
    cand = jnp.where((lane >= first) & (lane < first + EXPERTS_PER_GROUP), lg, -jnp.inf)
    v1, i1 = first_max(cand)
    v2, i2 = first_max(jnp.where(lane == i1, -jnp.inf, cand))
    t = jnp.exp(v2 - v1)
    p1 = 1.0 / (1.0 + t)
    wt_ref[...] = jnp.where(lane == i1, p1 * g_w, 0.0) + jnp.where(lane == i2, t * p1 * g_w, 0.0)


def _router(x, g, sc, sh, w_r, b_r, tm, rows_per_mod):
    n, d = x.shape
    r = sc.shape[1]
    whi, wlo = _split_bf16(w_r)
    mod_spec = pl.BlockSpec((None, r, d), lambda i: (i // rows_per_mod, 0, 0))
    full = lambda i: (0, 0)
    return pl.pallas_call(
        _router_kernel,
        grid=(n // tm,),
        in_specs=[
            pl.BlockSpec((tm, d), lambda i: (i, 0)),
            pl.BlockSpec((1, d), full),
            mod_spec,
            mod_spec,
            pl.BlockSpec((d, ROUTER_PAD), full),
            pl.BlockSpec((d, ROUTER_PAD), full),
            pl.BlockSpec((1, ROUTER_PAD), full),
        ],
        out_specs=[
            pl.BlockSpec((tm, d), lambda i: (i, 0)),
            pl.BlockSpec((tm, ROUTER_PAD), lambda i: (i, 0)),
        ],
        out_shape=[
            jax.ShapeDtypeStruct((n, d), BF16),
            jax.ShapeDtypeStruct((n, ROUTER_PAD), F32),
        ],
        compiler_params=_params("parallel"),
        name="router",
    )(x, g, sc, sh, whi, wlo, b_r)


def _moe_kernel(h_ref, wt_ref, wg_ref, wu_ref, wd_ref, x_ref, gt_ref, o_ref, acc_scr):
    e = pl.program_id(1)

    @pl.when(e == 0)
    def _():
        acc_scr[...] = jnp.zeros_like(acc_scr)

    h = h_ref[...]
    lane = lax.broadcasted_iota(jnp.int32, wt_ref.shape, 1)
    we = jnp.sum(jnp.where(lane == e + N_GROUPS, wt_ref[...], 0.0), axis=1, keepdims=True)
    a = _dot(h, wg_ref[...])
    hid = a * (1.0 / (1.0 + jnp.exp(-a))) * _dot(h, wu_ref[...])
    hid = jnp.where(we != 0.0, hid * we, 0.0)
    acc_scr[...] += _dot(hid.astype(BF16), wd_ref[...])

    @pl.when(e == pl.num_programs(1) - 1)
    def _():
        o_ref[...] = x_ref[...] + gt_ref[...] * acc_scr[...]


def _moe(h, wt, layer, wg, wu, wd, x, gt, tm, rows_per_mod):
    n, d = x.shape
    r = gt.shape[1]
    row = lambda i, e: (i, 0)
    return pl.pallas_call(
        _moe_kernel,
        grid=(n // tm, N_EXPERTS),
        in_specs=[
            pl.BlockSpec((tm, d), row),
            pl.BlockSpec((tm, ROUTER_PAD), row),
            pl.BlockSpec((None, None, d, D_EXPERT), lambda i, e: (layer, e, 0, 0)),
            pl.BlockSpec((None, None, d, D_EXPERT), lambda i, e: (layer, e, 0, 0)),
            pl.BlockSpec((None, None, D_EXPERT, d), lambda i, e: (layer, e, 0, 0)),
            pl.BlockSpec((tm, d), row),
            pl.BlockSpec((None, r, d), lambda i, e: (i // rows_per_mod, 0, 0)),
        ],
        out_specs=pl.BlockSpec((tm, d), row),
        out_shape=jax.ShapeDtypeStruct((n, d), F32),
        scratch_shapes=[pltpu.VMEM((tm, d), F32)],
        compiler_params=_params("parallel", "arbitrary"),
        name="moe",
    )(h, wt, wg, wu, wd, x, gt)


def _layer_weights(l, w_in, qn_g, kn_g, w_out, w_rg, b_rg, w_re, b_re, experts):
    d = w_in.shape[1]
    offs = np.cumsum([0, H_A * DH_A, H_A * DH_A, H_A * DH_A, H_B * DK_B, H_B * DK_B, H_B * DV_B,
                      GATE_RANK, D_MODEL, D_MODEL])
    wl = w_in[l]
    seg = [wl[:, offs[i]:offs[i + 1]] for i in range(9)]
    qa, ka, va, qb, kb, vb, al, ga, gb = seg
    al = jnp.pad(al, ((0, 0), (0, AL_PAD - GATE_RANK)))
    w_proj = jnp.concatenate([qa, qb, kb, vb, ga, gb, al], axis=1).astype(BF16)
    colgain = jnp.concatenate([jnp.tile(qn_g[l], H_A), jnp.ones((N_COLS - COL_QB,), F32)]).reshape(1, N_COLS)
    w_kv = jnp.concatenate([ka, va], axis=1).astype(BF16)
    kv_gain = jnp.concatenate([jnp.tile(kn_g[l], H_A), jnp.ones((D_MODEL,), F32)]).reshape(1, 2 * D_MODEL)
    pad = ROUTER_PAD - N_GROUPS - N_EXPERTS
    w_r = jnp.pad(jnp.concatenate([w_rg[l], w_re[l]], axis=1), ((0, 0), (0, pad)))
    b_r = jnp.pad(jnp.concatenate([b_rg[l], b_re[l]]), (0, pad)).reshape(1, ROUTER_PAD)
    return dict(w_proj=w_proj, colgain=colgain, w_kv=w_kv, kv_gain=kv_gain,
                w_out=w_out[l].astype(BF16), w_r=w_r, b_r=b_r, layer=l, experts=experts)


def _trunk_layer(x, mod, lw, sb_fn, gla_fn, norm1_g, norm2_g, tm, rows_per_mod):
    sh1, sc1, gt1, sh2, sc2, gt2 = mod
    up = PROJ_ROW_SCALE if rows_per_mod % PROJ_ROW_SCALE == 0 else 1
    proj = _proj(x, norm1_g, sc1, sh1, lw["w_proj"], lw["colgain"], tm * up, rows_per_mod // up)
    k, v = _proj_kv(x, norm1_g, sc1, sh1, lw["w_kv"], lw["kv_gain"], tm * up, rows_per_mod // up)
    o_a = sb_fn(proj, k, v)
    o_b, s_new = gla_fn(proj)
    tm_mix = min(tm, 512)
    x = _mix_out(proj, o_a, o_b, lw["w_out"], x, gt1, tm_mix, rows_per_mod * (tm // tm_mix))
    h, wt = _router(x, norm2_g, sc2, sh2, lw["w_r"], lw["b_r"], tm, rows_per_mod)
    x = _moe(h, wt, lw["layer"], *lw["experts"], x, gt2, tm, rows_per_mod)
    return x, k, v, s_new


def kernel(x_prompt, x_sample, cache_k, cache_v, state_gla, page_table, c_prompt, c_sample,
           norm1_g, norm2_g, w_ada, b_ada, w_in, qn_g, kn_g, sb_bias, w_gk2, b_gk2, onorm_g, w_out,
           w_rg, b_rg, w_re, b_re, w_e_gate, w_e_up, w_e_down):
    depth = w_in.shape[0]
    bp, seq, d = x_prompt.shape
    bs, t_s, _ = x_sample.shape
    n_pool, page = cache_k.shape[1], cache_k.shape[2]
    n_p, n_s = bp * seq, bs * t_s
    tm_p = min(1024, seq)
    assert seq % tm_p == 0 and n_s % SUBLANE == 0

    c_all = jnp.concatenate([c_prompt, c_sample], axis=0)
    r_pad = -c_all.shape[0] % SUBLANE
    mod_all = _ada(jnp.pad(c_all, ((0, r_pad), (0, 0))), w_ada, b_ada)

    ck = cache_k.reshape(depth, n_pool, page * H_A, DH_A)
    cv = cache_v.reshape(depth, n_pool, page * H_A, DH_A)
    gla_zero = jnp.zeros((1, bp, H_B, DK_B, DV_B), F32)
    experts = (w_e_gate.astype(BF16), w_e_up.astype(BF16), w_e_down.astype(BF16))

    yp = x_prompt.reshape(n_p, d)
    ys = x_sample.reshape(n_s, d)
    outs = [[] for _ in range(6)]
    for l in range(depth):
        lw = _layer_weights(l, w_in, qn_g, kn_g, w_out, w_rg, b_rg, w_re, b_re, experts)
        n1, n2 = norm1_g[l].reshape(1, d), norm2_g[l].reshape(1, d)
        mod_p = [m.reshape(bp, 1, d) for m in jnp.split(mod_all[l, :bp], 6, axis=-1)]
        mod_s = [jnp.repeat(m, t_s, axis=0).reshape(1, n_s, d)
                 for m in jnp.split(mod_all[l, bp:bp + bs], 6, axis=-1)]

        sb_p = functools.partial(_sb_prompt, bias=sb_bias[l], batch=bp, seq=seq)
        gla_p = functools.partial(_gla, w_gk2=w_gk2[l], b_gk2=b_gk2[l], onorm_g=onorm_g[l], s0=gla_zero,
                                  s0_layer=0, nb=bp, seq=seq, chunk=min(GLA_CHUNK, seq), t_valid=min(GLA_CHUNK, seq))
        yp, k_p, v_p, s_p = _trunk_layer(yp, mod_p, lw, sb_p, gla_p, n1, n2, tm_p, seq // tm_p)
        outs[0].append(k_p.reshape(bp, seq, H_A, DH_A))
        outs[1].append(v_p.reshape(bp, seq, H_A, DH_A))
        outs[2].append(s_p)

        def sb_s(proj, k, v, l=l):
            q = proj[:, COL_QA:COL_QB].reshape(bs, t_s, d)
            o = _sb_sample(l, q, k.reshape(bs, t_s, d), v.reshape(bs, t_s, d), ck, cv, page_table, sb_bias[l])
            return o.reshape(n_s, d)

        def gla_s(proj, l=l):
            p3 = jnp.pad(proj.reshape(bs, t_s, N_COLS), ((0, 0), (0, SUBLANE - t_s), (0, 0)))
            o, s_new = _gla(p3.reshape(bs * SUBLANE, N_COLS), w_gk2[l], b_gk2[l], onorm_g[l], state_gla, l,
                            nb=bs, seq=SUBLANE, chunk=SUBLANE, t_valid=t_s)
            return o.reshape(bs, SUBLANE, d)[:, :t_s].reshape(n_s, d), s_new

        ys, k_s, v_s, s_s = _trunk_layer(ys, mod_s, lw, sb_s, gla_s, n1, n2, n_s, 1)
        outs[3].append(k_s.reshape(bs, t_s, H_A, DH_A))
        outs[4].append(v_s.reshape(bs, t_s, H_A, DH_A))
        outs[5].append(s_s.astype(state_gla.dtype))

    return (yp.reshape(bp, seq, d), ys.reshape(bs, t_s, d), jnp.stack(outs[0]), jnp.stack(outs[1]),
            jnp.stack(outs[2]).astype(state_gla.dtype), jnp.stack(outs[3]), jnp.stack(outs[4]), jnp.stack(outs[5]))
```

```python
import functools

import jax
import jax.numpy as jnp
import numpy as np
from jax import lax
from jax.experimental import pallas as pl
from jax.experimental.pallas import tpu as pltpu

F32 = jnp.float32
BF16 = jnp.bfloat16

LANE = 128
SUBLANE = 8
VMEM_LIMIT = 56 * 1024 * 1024

D_MODEL = 1024
DH_A = 128
H_A = D_MODEL // DH_A
H_B = 4
DK_B = D_MODEL // (2 * H_B)
DV_B = D_MODEL // H_B
GATE_RANK = 16
GATE_NORM = 16.0
GLA_CHUNK = 128
N_GROUPS = 4
EXPERTS_PER_GROUP = 8
N_EXPERTS = N_GROUPS * EXPERTS_PER_GROUP
TOP_K = 2
D_EXPERT = D_MODEL // 4
EPS = 1e-6

COL_QA = 0
COL_QB = COL_QA + H_A * DH_A
COL_KB = COL_QB + H_B * DK_B
COL_VB = COL_KB + H_B * DK_B
COL_GA = COL_VB + H_B * DV_B
COL_GB = COL_GA + D_MODEL
COL_AL = COL_GB + D_MODEL
AL_PAD = 512
N_COLS = COL_AL + AL_PAD
PROJ_TN = 512
PROJ_ROW_SCALE = 2
ROUTER_PAD = LANE


def _params(*sem):
    return pltpu.CompilerParams(dimension_semantics=sem, vmem_limit_bytes=VMEM_LIMIT)


def _softplus(z):
    return jnp.maximum(z, 0.0) + jnp.log1p(jnp.exp(-jnp.abs(z)))


def _split_bf16(x):
    hi = x.astype(BF16)
    lo = (x - hi.astype(F32)).astype(BF16)
    return hi, lo


def _dot(a, b):
    return jnp.dot(a, b, preferred_element_type=F32)


def _dot_nt(a, b):
    return lax.dot_general(a, b, (((1,), (1,)), ((), ())), preferred_element_type=F32)


def _dot_tn(a, b):
    return lax.dot_general(a, b, (((0,), (0,)), ((), ())), preferred_element_type=F32)


def _rms(x):
    return x * lax.rsqrt(jnp.mean(x * x, axis=-1, keepdims=True) + EPS)


def _ada_kernel(c_ref, w_ref, b_ref, o_ref):
    c = c_ref[...]
    a = (c * (1.0 / (1.0 + jnp.exp(-c)))).astype(BF16)
    o_ref[...] = _dot(a, w_ref[...].astype(BF16)) + b_ref[...]


def _ada(c, w_ada, b_ada):
    depth, d, n = w_ada.shape
    r = c.shape[0]
    tn = 1024
    return pl.pallas_call(
        _ada_kernel,
        grid=(depth, n // tn),
        in_specs=[
            pl.BlockSpec((r, d), lambda l, j: (0, 0)),
            pl.BlockSpec((None, d, tn), lambda l, j: (l, 0, j)),
            pl.BlockSpec((None, 1, tn), lambda l, j: (l, 0, j)),
        ],
        out_specs=pl.BlockSpec((None, r, tn), lambda l, j: (l, 0, j)),
        out_shape=jax.ShapeDtypeStruct((depth, r, n), F32),
        compiler_params=_params("parallel", "parallel"),
        name="ada",
    )(c, w_ada, b_ada.reshape(depth, 1, n))


def _modulate_rows(x_ref, g_ref, sc_ref, sh_ref, h_scr):
    @pl.when(pl.program_id(1) == 0)
    def _():
        h = _rms(x_ref[...]) * g_ref[...]
        h_scr[...] = (h * (1.0 + sc_ref[...]) + sh_ref[...]).astype(BF16)


def _project_tile(h_scr, w_ref, cg_ref, o_ref, head_norm):
    acc = _dot(h_scr[...], w_ref[...])
    if not head_norm:
        o_ref[...] = acc
        return
    for c in range(acc.shape[1] // DH_A):
        sl = slice(c * DH_A, (c + 1) * DH_A)
        o_ref[:, sl] = _rms(acc[:, sl]) * cg_ref[:, sl]


def _proj_kernel(x_ref, g_ref, sc_ref, sh_ref, w_ref, cg_ref, o_ref, h_scr, *, n_norm_tiles):
    j = pl.program_id(1)
    _modulate_rows(x_ref, g_ref, sc_ref, sh_ref, h_scr)

    @pl.when(j < n_norm_tiles)
    def _():
        _project_tile(h_scr, w_ref, cg_ref, o_ref, True)

    @pl.when(j >= n_norm_tiles)
    def _():
        _project_tile(h_scr, w_ref, cg_ref, o_ref, False)


def _kv_kernel(x_ref, g_ref, sc_ref, sh_ref, w_ref, cg_ref, k_all_ref, v_all_ref, k_ref, v_ref, h_scr,
               *, n_k_tiles):
    del k_all_ref, v_all_ref
    j = pl.program_id(1)
    _modulate_rows(x_ref, g_ref, sc_ref, sh_ref, h_scr)

    @pl.when(j < n_k_tiles)
    def _():
        _project_tile(h_scr, w_ref, cg_ref, k_ref, True)

    @pl.when(j >= n_k_tiles)
    def _():
        _project_tile(h_scr, w_ref, cg_ref, v_ref, False)


def _proj_call(kernel_fn, x, g, sc, sh, w, colgain, tm, rows_per_mod, out_specs, out_shape, name, carried=()):
    n, d = x.shape
    r = sc.shape[1]
    tn = PROJ_TN
    mod_spec = pl.BlockSpec((None, r, d), lambda i, j: (i // rows_per_mod, 0, 0))
    in_specs = [
        pl.BlockSpec((tm, d), lambda i, j: (i, 0)),
        pl.BlockSpec((1, d), lambda i, j: (0, 0)),
        mod_spec,
        mod_spec,
        pl.BlockSpec((d, tn), lambda i, j: (0, j)),
        pl.BlockSpec((1, tn), lambda i, j: (0, j)),
    ]
    n_in = len(in_specs)
    return pl.pallas_call(
        kernel_fn,
        grid=(n // tm, w.shape[1] // tn),
        in_specs=in_specs + [pl.BlockSpec(memory_space=pl.ANY)] * len(carried),
        out_specs=out_specs,
        out_shape=out_shape,
        input_output_aliases={n_in + i: i for i in range(len(carried))},
        scratch_shapes=[pltpu.VMEM((tm, d), BF16)],
        compiler_params=_params("parallel", "arbitrary"),
        name=name,
    )(x, g, sc, sh, w, colgain, *carried)


def _proj(x, g, sc, sh, w, colgain, tm, rows_per_mod):
    n = x.shape[0]
    tn = PROJ_TN
    return _proj_call(functools.partial(_proj_kernel, n_norm_tiles=COL_QB // tn), x, g, sc, sh, w, colgain,
                      tm, rows_per_mod, pl.BlockSpec((tm, tn), lambda i, j: (i, j)),
                      jax.ShapeDtypeStruct((n, N_COLS), F32), "proj")


def _proj_kv(x, g, sc, sh, w, colgain, tm, rows_per_mod, layer, k_all, v_all):
    d = x.shape[1]
    tn = PROJ_TN
    nk = d // tn
    out = jax.ShapeDtypeStruct(k_all.shape, F32)
    return _proj_call(functools.partial(_kv_kernel, n_k_tiles=nk), x, g, sc, sh, w, colgain, tm, rows_per_mod,
                      [pl.BlockSpec((None, tm, tn), lambda i, j: (layer, i, jnp.minimum(j, nk - 1))),
                       pl.BlockSpec((None, tm, tn), lambda i, j: (layer, i, jnp.maximum(j - nk, 0)))],
                      [out, out], "proj_kv", carried=(k_all, v_all))


SB_HEADS_PER_STEP = 4


SB_TQ = 512
SB_TK = 256
LOG2E = 1.4426950408889634
SB_QSCALE = DH_A ** -0.5 * LOG2E


def _sb_softplus2(z):
    neg_abs = lax.bitcast_convert_type(lax.bitcast_convert_type(z, jnp.uint32) | jnp.uint32(0x80000000), F32)
    return jnp.maximum(z, 0.0) + jnp.log(1.0 + jnp.exp2(neg_abs)) * LOG2E


def _sb_weights(z, sp, drop, u, run):
    t = z - sp - _dot(drop.astype(BF16), u)
    return jnp.exp2(t if run is None else t - run)


def _sb_prompt_kernel(bias_ref, q_ref, k_ref, v_ref, u_ref, o_ref, run_scr, *, tq, tk, nh):
    hg = pl.program_id(1)
    qi = pl.program_id(2)
    u = u_ref[...]
    heads = [slice(i * DH_A, (i + 1) * DH_A) for i in range(nh)]
    qs = [(q_ref[:, hs] * SB_QSCALE).astype(BF16) for hs in heads]
    bias = [bias_ref[hg * nh + i] * LOG2E for i in range(nh)]
    row = lax.broadcasted_iota(jnp.int32, (tq, tk), 0)
    col = lax.broadcasted_iota(jnp.int32, (tq, tk), 1)
    o_ref[...] = jnp.zeros_like(o_ref)
    run_scr[...] = jnp.zeros_like(run_scr)

    def block(off, visible, r0=0):
        for i, hs in enumerate(heads):
            k = k_ref[pl.ds(off, tk), hs].astype(BF16)
            v = v_ref[pl.ds(off, tk), hs].astype(BF16)
            z = _dot_nt(qs[i][r0:], k) + bias[i]
            sp = _sb_softplus2(z)
            drop = sp if visible is None else jnp.where(visible[r0:], sp, 0.0)
            w = _sb_weights(z, sp, drop, u, run_scr[i, r0:])
            if visible is not None:
                w = jnp.where(visible[r0:], w, 0.0)
            o_ref[r0:, hs] += _dot(w.astype(BF16), v)
            run_scr[i, r0:] += jnp.sum(drop, axis=1, keepdims=True)

    for d in reversed(range(tq // tk)):
        block(pl.multiple_of(qi * tq + d * tk, tk), col + d * tk < row, d * tk)

    def body(jj, carry):
        block(pl.multiple_of(qi * tq - (jj + 1) * tk, tk), None)
        return carry

    lax.fori_loop(0, qi * (tq // tk), body, 0)


def _strict_lower(n):
    r = np.arange(n)
    return jnp.asarray(r[:, None] > r[None, :], dtype=BF16)


def _sb_prompt(proj, k, v, layer, bias, batch, seq):
    tq, tk = min(SB_TQ, seq), SB_TK
    nq = seq // tq
    nh = SB_HEADS_PER_STEP
    w = nh * DH_A
    return pl.pallas_call(
        functools.partial(_sb_prompt_kernel, tq=tq, tk=tk, nh=nh),
        grid_spec=pltpu.PrefetchScalarGridSpec(
            num_scalar_prefetch=1,
            grid=(batch, H_A // nh, nq),
            in_specs=[
                pl.BlockSpec((tq, w), lambda b, h, i, s: (b * nq + i, COL_QA // w + h)),
                pl.BlockSpec((None, seq, w), lambda b, h, i, s: (layer, b, h)),
                pl.BlockSpec((None, seq, w), lambda b, h, i, s: (layer, b, h)),
                pl.BlockSpec((tk, tk), lambda b, h, i, s: (0, 0)),
            ],
            out_specs=pl.BlockSpec((tq, w), lambda b, h, i, s: (b * nq + i, h)),
            scratch_shapes=[pltpu.VMEM((nh, tq, 1), F32)],
        ),
        out_shape=jax.ShapeDtypeStruct((batch * seq, D_MODEL), F32),
        compiler_params=_params("parallel", "parallel", "arbitrary"),
        name="sb_prompt",
    )(bias, proj, k, v, _strict_lower(tk))


def _sb_sample_kernel(pt_ref, q_ref, kn_ref, vn_ref, *rest, n_pages, page):
    kc_refs, vc_refs = rest[:n_pages], rest[n_pages:2 * n_pages]
    bias_ref, u_ref, o_ref, kpad, vpad = rest[2 * n_pages:]
    heads = [slice(h * DH_A, (h + 1) * DH_A) for h in range(H_A)]
    qs = [(q_ref[:, hs] * SB_QSCALE).astype(BF16) for hs in heads]
    rows = H_A * SUBLANE

    def block(k_of, v_of, width, before):
        z = jnp.concatenate([_dot_nt(q, k_of(h)) for h, q in enumerate(qs)], axis=0) + bias_ref[:, 0:width]
        sp = _sb_softplus2(z)
        drop = sp if before is None else jnp.where(before, sp, 0.0)
        w = _sb_weights(z, sp, drop, u_ref[0:width, 0:width], None)
        if before is not None:
            w = jnp.where(before, w, 0.0)
        pv = [_dot(w[h * SUBLANE:(h + 1) * SUBLANE].astype(BF16), v_of(h)) for h in range(H_A)]
        return pv, jnp.sum(drop, axis=1, keepdims=True)

    def cached(refs, ps):
        return lambda h: jnp.concatenate([refs[p][pl.ds(h, page, stride=H_A), :] for p in ps], axis=0).astype(BF16)

    kpad[...] = jnp.zeros_like(kpad)
    vpad[...] = jnp.zeros_like(vpad)
    kpad[0:SUBLANE, :] = kn_ref[...]
    vpad[0:SUBLANE, :] = vn_ref[...]
    t_of_row = lax.broadcasted_iota(jnp.int32, (rows, page), 0) & (SUBLANE - 1)
    before = lax.broadcasted_iota(jnp.int32, (rows, page), 1) < t_of_row
    blocks = [block(lambda h: kpad[:, heads[h]].astype(BF16), lambda h: vpad[:, heads[h]].astype(BF16),
                    page, before)]
    for hi_page in range(n_pages - 1, -1, -2):
        ps = [p for p in (hi_page - 1, hi_page) if p >= 0]
        blocks.append(block(cached(kc_refs, ps), cached(vc_refs, ps), len(ps) * page, None))
    run = jnp.zeros((rows, 1), F32)
    acc = [jnp.zeros((SUBLANE, DH_A), F32) for _ in range(H_A)]
    for pv, row_drop in blocks:
        scale = jnp.exp2(-run)
        acc = [acc[h] + scale[h * SUBLANE:(h + 1) * SUBLANE] * pv[h] for h in range(H_A)]
        run = run + row_drop
    o_ref[...] = jnp.concatenate(acc, axis=1)[0:o_ref.shape[0]]


def _sb_sample(layer, q, k_new, v_new, cache_k, cache_v, page_table, bias):
    nb, t_new, d = q.shape
    n_pages = page_table.shape[1]
    page = cache_k.shape[2] // H_A
    assert page == LANE and t_new <= SUBLANE
    pad_rows = ((0, 0), (0, SUBLANE - t_new), (0, 0))
    rows = H_A * SUBLANE
    wide = 2 * page
    bias_t = jnp.broadcast_to(jnp.repeat(bias * LOG2E, SUBLANE)[:, None], (rows, wide))
    row_spec = pl.BlockSpec((None, SUBLANE, d), lambda b, pt: (b, 0, 0))

    def page_spec(p):
        return pl.BlockSpec((None, None, page * H_A, DH_A), lambda b, pt: (layer, pt[b, p], 0, 0))

    pages = [page_spec(p) for p in range(n_pages)]
    return pl.pallas_call(
        functools.partial(_sb_sample_kernel, n_pages=n_pages, page=page),
        grid_spec=pltpu.PrefetchScalarGridSpec(
            num_scalar_prefetch=1,
            grid=(nb,),
            in_specs=[row_spec, row_spec, row_spec] + pages + pages + [
                pl.BlockSpec((rows, wide), lambda b, pt: (0, 0)),
                pl.BlockSpec((wide, wide), lambda b, pt: (0, 0)),
            ],
            out_specs=pl.BlockSpec((None, t_new, d), lambda b, pt: (b, 0, 0)),
            scratch_shapes=[pltpu.VMEM((page, d), F32), pltpu.VMEM((page, d), F32)],
        ),
        out_shape=jax.ShapeDtypeStruct((nb, t_new, d), F32),
        compiler_params=_params("parallel"),
        name="sb_sample",
    )(page_table, jnp.pad(q, pad_rows), jnp.pad(k_new, pad_rows), jnp.pad(v_new, pad_rows),
      *([cache_k] * n_pages), *([cache_v] * n_pages), bias_t, _strict_lower(wide))


def _gla_kernel(q_ref, k_ref, v_ref, al_ref, wg_ref, bg_ref, on_ref, s0_ref, tri_ref, s_all_ref, o_ref,
                sfin_ref, s_scr, *, chunk, t_valid):
    del s_all_ref
    c = pl.program_id(1)

    @pl.when(c == 0)
    def _():
        s_scr[...] = s0_ref[...]

    tri = tri_ref[...]
    lower = lax.broadcasted_iota(jnp.int32, (chunk, chunk), 0) >= lax.broadcasted_iota(jnp.int32, (chunk, chunk), 1)
    ones = jnp.ones((chunk, DK_B), BF16)
    for b in range(s_scr.shape[0]):
        gate = _dot(al_ref[b].astype(BF16), wg_ref[...]) + bg_ref[...]
        log_a = -_softplus(-gate) * (1.0 / GATE_NORM)
        if t_valid < chunk:
            rows = lax.broadcasted_iota(jnp.int32, log_a.shape, 0)
            log_a = jnp.where(rows < t_valid, log_a, 0.0)
        for h in range(H_B):
            ks = slice(h * DK_B, (h + 1) * DK_B)
            vs = slice(h * DV_B, (h + 1) * DV_B)
            hi, lo = _split_bf16(log_a[:, ks])
            cum = _dot(tri, hi) + _dot(tri, lo)
            last = cum[chunk - 1:chunk, :]
            last_col = jnp.exp(_dot_tn(hi, ones) + _dot_tn(lo, ones))
            q = q_ref[b, :, ks] * (DK_B ** -0.5)
            k = k_ref[b, :, ks]
            v = v_ref[b, :, vs].astype(BF16)
            q_dec = (q * jnp.exp(cum)).astype(BF16)
            mid = cum[chunk // 2:chunk // 2 + 1, :]
            q_mid = (q * jnp.exp(cum - mid)).astype(BF16)
            k_mid = (k * jnp.exp(mid - cum)).astype(BF16)
            att = jnp.where(lower, _dot_nt(q_mid, k_mid), 0.0)
            s = s_scr[b, h]
            o = _dot(q_dec, s.astype(BF16)) + _dot(att.astype(BF16), v)
            k_rem = (k * jnp.exp(last - cum)).astype(BF16)
            s_scr[b, h] = jnp.concatenate([last_col] * (DV_B // DK_B), axis=1) * s + _dot_tn(k_rem, v)
            o_ref[b, :, vs] = _rms(o) * on_ref[...]

    @pl.when(c == pl.num_programs(1) - 1)
    def _():
        sfin_ref[...] = s_scr[...]


GLA_SEQS_PER_STEP = 4


def _gla(proj, w_gk2, b_gk2, onorm_g, s0, s0_layer, s_all, layer, nb, seq, chunk, t_valid):
    nc = seq // chunk
    ns = GLA_SEQS_PER_STEP
    assert nb % ns == 0
    wg = jnp.zeros((LANE, H_B * DK_B), F32).at[:GATE_RANK].set(w_gk2).astype(BF16)
    r = np.arange(chunk)
    tri = jnp.asarray(r[:, None] >= r[None, :], dtype=BF16)
    kw = H_B * DK_B
    vw = H_B * DV_B
    p3 = proj.reshape(nb, seq, N_COLS)
    state_spec = pl.BlockSpec((None, ns, H_B, DK_B, DV_B), lambda b, c: (layer, b, 0, 0, 0))
    o, s_fin = pl.pallas_call(
        functools.partial(_gla_kernel, chunk=chunk, t_valid=t_valid),
        grid=(nb // ns, nc),
        in_specs=[
            pl.BlockSpec((ns, chunk, kw), lambda b, c: (b, c, COL_QB // kw)),
            pl.BlockSpec((ns, chunk, kw), lambda b, c: (b, c, COL_KB // kw)),
            pl.BlockSpec((ns, chunk, vw), lambda b, c: (b, c, COL_VB // vw)),
            pl.BlockSpec((ns, chunk, LANE), lambda b, c: (b, c, COL_AL // LANE)),
            pl.BlockSpec((LANE, kw), lambda b, c: (0, 0)),
            pl.BlockSpec((1, kw), lambda b, c: (0, 0)),
            pl.BlockSpec((1, DV_B), lambda b, c: (0, 0)),
            pl.BlockSpec((None, ns, H_B, DK_B, DV_B), lambda b, c: (s0_layer, b, 0, 0, 0)),
            pl.BlockSpec((chunk, chunk), lambda b, c: (0, 0)),
            pl.BlockSpec(memory_space=pl.ANY),
        ],
        out_specs=[
            pl.BlockSpec((ns, chunk, vw), lambda b, c: (b, c, 0)),
            state_spec,
        ],
        out_shape=[
            jax.ShapeDtypeStruct((nb, seq, vw), F32),
            jax.ShapeDtypeStruct(s_all.shape, F32),
        ],
        input_output_aliases={9: 1},
        scratch_shapes=[pltpu.VMEM((ns, H_B, DK_B, DV_B), F32)],
        compiler_params=_params("parallel", "arbitrary"),
        name="gla",
    )(p3, p3, p3, p3, wg, b_gk2.reshape(1, kw), onorm_g.reshape(1, DV_B), s0, tri, s_all)
    return o.reshape(nb * seq, vw), s_fin


def _mix_kernel(ga_ref, gb_ref, oa_ref, ob_ref, w_ref, x_ref, gt_ref, o_ref):
    def sig(t):
        return 1.0 / (1.0 + jnp.exp(-t))

    merged = sig(ga_ref[...]) * oa_ref[...] + sig(gb_ref[...]) * ob_ref[...]
    o_ref[...] = x_ref[...] + gt_ref[...] * _dot(merged.astype(BF16), w_ref[...])


def _mix_out(proj, o_a, o_b, w_out, x, gt, tm, rows_per_mod):
    n, d = x.shape
    r = gt.shape[1]
    row = lambda i: (i, 0)
    return pl.pallas_call(
        _mix_kernel,
        grid=(n // tm,),
        in_specs=[
            pl.BlockSpec((tm, d), lambda i: (i, COL_GA // d)),
            pl.BlockSpec((tm, d), lambda i: (i, COL_GB // d)),
            pl.BlockSpec((tm, d), row),
            pl.BlockSpec((tm, d), row),
            pl.BlockSpec((d, d), lambda i: (0, 0)),
            pl.BlockSpec((tm, d), row),
            pl.BlockSpec((None, r, d), lambda i: (i // rows_per_mod, 0, 0)),
        ],
        out_specs=pl.BlockSpec((tm, d), row),
        out_shape=jax.ShapeDtypeStruct((n, d), F32),
        compiler_params=_params("parallel"),
        name="mix_out",
    )(proj, proj, o_a, o_b, w_out, x, gt)


def _router_kernel(x_ref, g_ref, sc_ref, sh_ref, whi_ref, wlo_ref, b_ref, h_ref, wt_ref):
    h = _rms(x_ref[...]) * g_ref[...]
    h = h * (1.0 + sc_ref[...]) + sh_ref[...]
    h_ref[...] = h.astype(BF16)
    hi, lo = _split_bf16(h)
    lg = _dot(hi, whi_ref[...]) + (_dot(lo, whi_ref[...]) + _dot(hi, wlo_ref[...])) + b_ref[...]

    lane = lax.broadcasted_iota(jnp.int32, lg.shape, 1).astype(F32)
    past_end = float(ROUTER_PAD)

    def first_max(vals):
        m = jnp.max(vals, axis=1, keepdims=True)
        return m, jnp.min(jnp.where(vals == m, lane, past_end), axis=1, keepdims=True)

    is_group = lane < N_GROUPS
    g_max, g_idx = first_max(jnp.where(is_group, lg, -jnp.inf))
    g_w = 1.0 / jnp.sum(jnp.where(is_group, jnp.exp(lg - g_max), 0.0), axis=1, keepdims=True)
    first = N_GROUPS + g_idx * EXPERTS_PER_GROUP
    cand = jnp.where((lane >= first) & (lane < first + EXPERTS_PER_GROUP), lg, -jnp.inf)
    v1, i1 = first_max(cand)
    v2, i2 = first_max(jnp.where(lane == i1, -jnp.inf, cand))
    t = jnp.exp(v2 - v1)
    p1 = 1.0 / (1.0 + t)
    wt_ref[...] = jnp.where(lane == i1, p1 * g_w, 0.0) + jnp.where(lane == i2, t * p1 * g_w, 0.0)


def _router(x, g, sc, sh, w_r, b_r, tm, rows_per_mod):
    n, d = x.shape
    r = sc.shape[1]
    whi, wlo = _split_bf16(w_r)
    mod_spec = pl.BlockSpec((None, r, d), lambda i: (i // rows_per_mod, 0, 0))
    full = lambda i: (0, 0)
    return pl.pallas_call(
        _router_kernel,
        grid=(n // tm,),
        in_specs=[
            pl.BlockSpec((tm, d), lambda i: (i, 0)),
            pl.BlockSpec((1, d), full),
            mod_spec,
            mod_spec,
            pl.BlockSpec((d, ROUTER_PAD), full),
            pl.BlockSpec((d, ROUTER_PAD), full),
            pl.BlockSpec((1, ROUTER_PAD), full),
        ],
        out_specs=[
            pl.BlockSpec((tm, d), lambda i: (i, 0)),
            pl.BlockSpec((tm, ROUTER_PAD), lambda i: (i, 0)),
        ],
        out_shape=[
            jax.ShapeDtypeStruct((n, d), BF16),
            jax.ShapeDtypeStruct((n, ROUTER_PAD), F32),
        ],
        compiler_params=_params("parallel"),
        name="router",
    )(x, g, sc, sh, whi, wlo, b_r)


def _moe_kernel(h_ref, wt_ref, wg_ref, wu_ref, wd_ref, x_ref, gt_ref, o_ref, acc_scr):
    e = pl.program_id(1)

    @pl.when(e == 0)
    def _():
        acc_scr[...] = jnp.zeros_like(acc_scr)

    h = h_ref[...]
    lane = lax.broadcasted_iota(jnp.int32, wt_ref.shape, 1)
    we = jnp.sum(jnp.where(lane == e + N_GROUPS, wt_ref[...], 0.0), axis=1, keepdims=True)
    a = _dot(h, wg_ref[...])
    hid = a * (1.0 / (1.0 + jnp.exp(-a))) * _dot(h, wu_ref[...])
    hid = jnp.where(we != 0.0, hid * we, 0.0)
    acc_scr[...] += _dot(hid.astype(BF16), wd_ref[...])

    @pl.when(e == pl.num_programs(1) - 1)
    def _():
        o_ref[...] = x_ref[...] + gt_ref[...] * acc_scr[...]


def _moe(h, wt, layer, wg, wu, wd, x, gt, tm, rows_per_mod):
    n, d = x.shape
    r = gt.shape[1]
    row = lambda i, e: (i, 0)
    return pl.pallas_call(
        _moe_kernel,
        grid=(n // tm, N_EXPERTS),
        in_specs=[
            pl.BlockSpec((tm, d), row),
            pl.BlockSpec((tm, ROUTER_PAD), row),
            pl.BlockSpec((None, None, d, D_EXPERT), lambda i, e: (layer, e, 0, 0)),
            pl.BlockSpec((None, None, d, D_EXPERT), lambda i, e: (layer, e, 0, 0)),
            pl.BlockSpec((None, None, D_EXPERT, d), lambda i, e: (layer, e, 0, 0)),
            pl.BlockSpec((tm, d), row),
            pl.BlockSpec((None, r, d), lambda i, e: (i // rows_per_mod, 0, 0)),
        ],
        out_specs=pl.BlockSpec((tm, d), row),
        out_shape=jax.ShapeDtypeStruct((n, d), F32),
        scratch_shapes=[pltpu.VMEM((tm, d), F32)],
        compiler_params=_params("parallel", "arbitrary"),
        name="moe",
    )(h, wt, wg, wu, wd, x, gt)


def _layer_weights(l, w_in, qn_g, kn_g, w_out, w_rg, b_rg, w_re, b_re, experts):
    d = w_in.shape[1]
    offs = np.cumsum([0, H_A * DH_A, H_A * DH_A, H_A * DH_A, H_B * DK_B, H_B * DK_B, H_B * DV_B,
                      GATE_RANK, D_MODEL, D_MODEL])
    wl = w_in[l]
    seg = [wl[:, offs[i]:offs[i + 1]] for i in range(9)]
    qa, ka, va, qb, kb, vb, al, ga, gb = seg
    al = jnp.pad(al, ((0, 0), (0, AL_PAD - GATE_RANK)))
    w_proj = jnp.concatenate([qa, qb, kb, vb, ga, gb, al], axis=1).astype(BF16)
    colgain = jnp.concatenate([jnp.tile(qn_g[l], H_A), jnp.ones((N_COLS - COL_QB,), F32)]).reshape(1, N_COLS)
    w_kv = jnp.concatenate([ka, va], axis=1).astype(BF16)
    kv_gain = jnp.concatenate([jnp.tile(kn_g[l], H_A), jnp.ones((D_MODEL,), F32)]).reshape(1, 2 * D_MODEL)
    pad = ROUTER_PAD - N_GROUPS - N_EXPERTS
    w_r = jnp.pad(jnp.concatenate([w_rg[l], w_re[l]], axis=1), ((0, 0), (0, pad)))
    b_r = jnp.pad(jnp.concatenate([b_rg[l], b_re[l]]), (0, pad)).reshape(1, ROUTER_PAD)
    return dict(w_proj=w_proj, colgain=colgain, w_kv=w_kv, kv_gain=kv_gain,
                w_out=w_out[l].astype(BF16), w_r=w_r, b_r=b_r, layer=l, experts=experts)


def _trunk_layer(x, carried, mod, lw, sb_fn, gla_fn, norm1_g, norm2_g, tm, rows_per_mod):
    sh1, sc1, gt1, sh2, sc2, gt2 = mod
    k_all, v_all, s_all = carried
    up = PROJ_ROW_SCALE if rows_per_mod % PROJ_ROW_SCALE == 0 else 1
    proj = _proj(x, norm1_g, sc1, sh1, lw["w_proj"], lw["colgain"], tm * up, rows_per_mod // up)
    k_all, v_all = _proj_kv(x, norm1_g, sc1, sh1, lw["w_kv"], lw["kv_gain"], tm * up, rows_per_mod // up,
                            lw["layer"], k_all, v_all)
    o_a = sb_fn(proj, k_all, v_all)
    o_b, s_all = gla_fn(proj, s_all)
    tm_mix = min(tm, 512)
    x = _mix_out(proj, o_a, o_b, lw["w_out"], x, gt1, tm_mix, rows_per_mod * (tm // tm_mix))
    h, wt = _router(x, norm2_g, sc2, sh2, lw["w_r"], lw["b_r"], tm, rows_per_mod)
    x = _moe(h, wt, lw["layer"], *lw["experts"], x, gt2, tm, rows_per_mod)
    return x, (k_all, v_all, s_all)


def kernel(x_prompt, x_sample, cache_k, cache_v, state_gla, page_table, c_prompt, c_sample,
           norm1_g, norm2_g, w_ada, b_ada, w_in, qn_g, kn_g, sb_bias, w_gk2, b_gk2, onorm_g, w_out,
           w_rg, b_rg, w_re, b_re, w_e_gate, w_e_up, w_e_down):
    depth = w_in.shape[0]
    bp, seq, d = x_prompt.shape
    bs, t_s, _ = x_sample.shape
    n_pool, page = cache_k.shape[1], cache_k.shape[2]
    n_p, n_s = bp * seq, bs * t_s
    tm_p = min(1024, seq)
    assert seq % tm_p == 0 and n_s % SUBLANE == 0

    c_all = jnp.concatenate([c_prompt, c_sample], axis=0)
    r_pad = -c_all.shape[0] % SUBLANE
    mod_all = _ada(jnp.pad(c_all, ((0, r_pad), (0, 0))), w_ada, b_ada)

    ck = cache_k.reshape(depth, n_pool, page * H_A, DH_A)
    cv = cache_v.reshape(depth, n_pool, page * H_A, DH_A)
    gla_zero = jnp.zeros((1, bp, H_B, DK_B, DV_B), F32)
    experts = (w_e_gate.astype(BF16), w_e_up.astype(BF16), w_e_down.astype(BF16))

    yp = x_prompt.reshape(n_p, d)
    ys = x_sample.reshape(n_s, d)
    state_shape = (H_B, DK_B, DV_B)
    car_p = (jnp.zeros((depth, n_p, d), F32), jnp.zeros((depth, n_p, d), F32),
             jnp.zeros((depth, bp) + state_shape, F32))
    car_s = (jnp.zeros((depth, n_s, d), F32), jnp.zeros((depth, n_s, d), F32),
             jnp.zeros((depth, bs) + state_shape, F32))
    for l in range(depth):
        lw = _layer_weights(l, w_in, qn_g, kn_g, w_out, w_rg, b_rg, w_re, b_re, experts)
        n1, n2 = norm1_g[l].reshape(1, d), norm2_g[l].reshape(1, d)
        mod_p = [m.reshape(bp, 1, d) for m in jnp.split(mod_all[l, :bp], 6, axis=-1)]
        mod_s = [jnp.repeat(m, t_s, axis=0).reshape(1, n_s, d)
                 for m in jnp.split(mod_all[l, bp:bp + bs], 6, axis=-1)]

        def sb_p(proj, k_all, v_all, l=l):
            return _sb_prompt(proj, k_all, v_all, l, sb_bias[l], bp, seq)

        def gla_p(proj, s_all, l=l):
            return _gla(proj, w_gk2[l], b_gk2[l], onorm_g[l], gla_zero, 0, s_all, l,
                        nb=bp, seq=seq, chunk=min(GLA_CHUNK, seq), t_valid=min(GLA_CHUNK, seq))

        yp, car_p = _trunk_layer(yp, car_p, mod_p, lw, sb_p, gla_p, n1, n2, tm_p, seq // tm_p)

        def sb_s(proj, k_all, v_all, l=l):
            q = proj[:, COL_QA:COL_QB].reshape(bs, t_s, d)
            o = _sb_sample(l, q, k_all[l].reshape(bs, t_s, d), v_all[l].reshape(bs, t_s, d), ck, cv,
                           page_table, sb_bias[l])
            return o.reshape(n_s, d)

        def gla_s(proj, s_all, l=l):
            p3 = jnp.pad(proj.reshape(bs, t_s, N_COLS), ((0, 0), (0, SUBLANE - t_s), (0, 0)))
            o, s_all = _gla(p3.reshape(bs * SUBLANE, N_COLS), w_gk2[l], b_gk2[l], onorm_g[l], state_gla, l,
                            s_all, l, nb=bs, seq=SUBLANE, chunk=SUBLANE, t_valid=t_s)
            return o.reshape(bs, SUBLANE, d)[:, :t_s].reshape(n_s, d), s_all

        ys, car_s = _trunk_layer(ys, car_s, mod_s, lw, sb_s, gla_s, n1, n2, n_s, 1)

    kv_p, kv_s = (depth, bp, seq, H_A, DH_A), (depth, bs, t_s, H_A, DH_A)
    return (yp.reshape(bp, seq, d), ys.reshape(bs, t_s, d), car_p[0].reshape(kv_p), car_p[1].reshape(kv_p),
            car_p[2].astype(state_gla.dtype), car_s[0].reshape(kv_s), car_s[1].reshape(kv_s),
            car_s[2].astype(state_gla.dtype))
```

```python
import functools

import jax
import jax.numpy as jnp
import numpy as np
from jax import lax
from jax.experimental import pallas as pl
from jax.experimental.pallas import tpu as pltpu

F32 = jnp.float32
BF16 = jnp.bfloat16

LANE = 128
SUBLANE = 8
VMEM_LIMIT = 56 * 1024 * 1024

D_MODEL = 1024
DH_A = 128
H_A = D_MODEL // DH_A
H_B = 4
DK_B = D_MODEL // (2 * H_B)
DV_B = D_MODEL // H_B
GATE_RANK = 16
GATE_NORM = 16.0
GLA_CHUNK = 128
N_GROUPS = 4
EXPERTS_PER_GROUP = 8
N_EXPERTS = N_GROUPS * EXPERTS_PER_GROUP
TOP_K = 2
D_EXPERT = D_MODEL // 4
EPS = 1e-6

COL_QA = 0
COL_QB = COL_QA + H_A * DH_A
COL_KB = COL_QB + H_B * DK_B
COL_VB = COL_KB + H_B * DK_B
COL_GA = COL_VB + H_B * DV_B
COL_GB = COL_GA + D_MODEL
COL_AL = COL_GB + D_MODEL
AL_PAD = 512
N_COLS = COL_AL + AL_PAD
PROJ_TN = 512
PROJ_ROW_SCALE = 2
ROUTER_PAD = LANE


def _params(*sem):
    return pltpu.CompilerParams(dimension_semantics=sem, vmem_limit_bytes=VMEM_LIMIT)


def _softplus(z):
    return jnp.maximum(z, 0.0) + jnp.log1p(jnp.exp(-jnp.abs(z)))


def _split_bf16(x):
    hi = x.astype(BF16)
    lo = (x - hi.astype(F32)).astype(BF16)
    return hi, lo


def _dot(a, b):
    return jnp.dot(a, b, preferred_element_type=F32)


def _dot_nt(a, b):
    return lax.dot_general(a, b, (((1,), (1,)), ((), ())), preferred_element_type=F32)


def _dot_tn(a, b):
    return lax.dot_general(a, b, (((0,), (0,)), ((), ())), preferred_element_type=F32)


def _rms(x):
    return x * lax.rsqrt(jnp.mean(x * x, axis=-1, keepdims=True) + EPS)


def _ada_kernel(c_ref, w_ref, b_ref, o_ref):
    c = c_ref[...]
    a = (c * (1.0 / (1.0 + jnp.exp(-c)))).astype(BF16)
    o_ref[...] = _dot(a, w_ref[...].astype(BF16)) + b_ref[...]


def _ada(c, w_ada, b_ada):
    depth, d, n = w_ada.shape
    r = c.shape[0]
    tn = 1024
    return pl.pallas_call(
        _ada_kernel,
        grid=(depth, n // tn),
        in_specs=[
            pl.BlockSpec((r, d), lambda l, j: (0, 0)),
            pl.BlockSpec((None, d, tn), lambda l, j: (l, 0, j)),
            pl.BlockSpec((None, 1, tn), lambda l, j: (l, 0, j)),
        ],
        out_specs=pl.BlockSpec((None, r, tn), lambda l, j: (l, 0, j)),
        out_shape=jax.ShapeDtypeStruct((depth, r, n), F32),
        compiler_params=_params("parallel", "parallel"),
        name="ada",
    )(c, w_ada, b_ada.reshape(depth, 1, n))


def _modulate_rows(x_ref, g_ref, sc_ref, sh_ref, h_scr):
    @pl.when(pl.program_id(1) == 0)
    def _():
        h = _rms(x_ref[...]) * g_ref[...]
        h_scr[...] = (h * (1.0 + sc_ref[...]) + sh_ref[...]).astype(BF16)


def _project_tile(h_scr, w_ref, cg_ref, o_ref, head_norm):
    acc = _dot(h_scr[...], w_ref[...])
    if not head_norm:
        o_ref[...] = acc
        return
    for c in range(acc.shape[1] // DH_A):
        sl = slice(c * DH_A, (c + 1) * DH_A)
        o_ref[:, sl] = _rms(acc[:, sl]) * cg_ref[:, sl]


def _proj_kernel(x_ref, g_ref, sc_ref, sh_ref, w_ref, cg_ref, k_all_ref, v_all_ref, o_ref, k_ref, v_ref,
                 h_scr, *, nq):
    del k_all_ref, v_all_ref
    j = pl.program_id(1)
    _modulate_rows(x_ref, g_ref, sc_ref, sh_ref, h_scr)

    @pl.when(j < nq)
    def _():
        _project_tile(h_scr, w_ref, cg_ref, o_ref, True)

    @pl.when((j >= nq) & (j < 2 * nq))
    def _():
        _project_tile(h_scr, w_ref, cg_ref, k_ref, True)

    @pl.when((j >= 2 * nq) & (j < 3 * nq))
    def _():
        _project_tile(h_scr, w_ref, cg_ref, v_ref, False)

    @pl.when(j >= 3 * nq)
    def _():
        _project_tile(h_scr, w_ref, cg_ref, o_ref, False)


def _proj(x, g, sc, sh, w, colgain, tm, rows_per_mod, layer, k_all, v_all):
    n, d = x.shape
    r = sc.shape[1]
    tn = PROJ_TN
    nq = d // tn
    mod_spec = pl.BlockSpec((None, r, d), lambda i, j: (i // rows_per_mod, 0, 0))
    in_specs = [
        pl.BlockSpec((tm, d), lambda i, j: (i, 0)),
        pl.BlockSpec((1, d), lambda i, j: (0, 0)),
        mod_spec,
        mod_spec,
        pl.BlockSpec((d, tn), lambda i, j: (0, j)),
        pl.BlockSpec((1, tn), lambda i, j: (0, j)),
        pl.BlockSpec(memory_space=pl.ANY),
        pl.BlockSpec(memory_space=pl.ANY),
    ]
    kv_shape = jax.ShapeDtypeStruct(k_all.shape, F32)
    return pl.pallas_call(
        functools.partial(_proj_kernel, nq=nq),
        grid=(n // tm, w.shape[1] // tn),
        in_specs=in_specs,
        out_specs=[
            pl.BlockSpec((tm, tn), lambda i, j: (i, jnp.where(j < nq, j, jnp.maximum(j - 2 * nq, nq - 1)))),
            pl.BlockSpec((None, tm, tn), lambda i, j: (layer, i, jnp.clip(j - nq, 0, nq - 1))),
            pl.BlockSpec((None, tm, tn), lambda i, j: (layer, i, jnp.clip(j - 2 * nq, 0, nq - 1))),
        ],
        out_shape=[jax.ShapeDtypeStruct((n, N_COLS), F32), kv_shape, kv_shape],
        input_output_aliases={6: 1, 7: 2},
        scratch_shapes=[pltpu.VMEM((tm, d), BF16)],
        compiler_params=_params("parallel", "arbitrary"),
        name="proj",
    )(x, g, sc, sh, w, colgain, k_all, v_all)


SB_HEADS_PER_STEP = 4


SB_TQ = 512
SB_TK = 256
LOG2E = 1.4426950408889634
SB_QSCALE = DH_A ** -0.5 * LOG2E


def _sb_softplus2(z):
    neg_abs = lax.bitcast_convert_type(lax.bitcast_convert_type(z, jnp.uint32) | jnp.uint32(0x80000000), F32)
    return jnp.maximum(z, 0.0) + jnp.log(1.0 + jnp.exp2(neg_abs)) * LOG2E


def _sb_weights(z, sp, drop, u, run):
    t = z - sp - _dot(drop.astype(BF16), u)
    return jnp.exp2(t if run is None else t - run)


def _sb_prompt_kernel(bias_ref, q_ref, k_ref, v_ref, u_ref, o_ref, run_scr, *, tq, tk, nh):
    hg = pl.program_id(1)
    qi = pl.program_id(2)
    u = u_ref[...]
    heads = [slice(i * DH_A, (i + 1) * DH_A) for i in range(nh)]
    qs = [(q_ref[:, hs] * SB_QSCALE).astype(BF16) for hs in heads]
    bias = [bias_ref[hg * nh + i] * LOG2E for i in range(nh)]
    row = lax.broadcasted_iota(jnp.int32, (tq, tk), 0)
    col = lax.broadcasted_iota(jnp.int32, (tq, tk), 1)
    o_ref[...] = jnp.zeros_like(o_ref)
    run_scr[...] = jnp.zeros_like(run_scr)

    def block(off, visible, r0=0):
        for i, hs in enumerate(heads):
            k = k_ref[pl.ds(off, tk), hs].astype(BF16)
            v = v_ref[pl.ds(off, tk), hs].astype(BF16)
            z = _dot_nt(qs[i][r0:], k) + bias[i]
            sp = _sb_softplus2(z)
            drop = sp if visible is None else jnp.where(visible[r0:], sp, 0.0)
            w = _sb_weights(z, sp, drop, u, run_scr[i, r0:])
            if visible is not None:
                w = jnp.where(visible[r0:], w, 0.0)
            o_ref[r0:, hs] += _dot(w.astype(BF16), v)
            run_scr[i, r0:] += jnp.sum(drop, axis=1, keepdims=True)

    for d in reversed(range(tq // tk)):
        block(pl.multiple_of(qi * tq + d * tk, tk), col + d * tk < row, d * tk)

    def body(jj, carry):
        block(pl.multiple_of(qi * tq - (jj + 1) * tk, tk), None)
        return carry

    lax.fori_loop(0, qi * (tq // tk), body, 0)


def _strict_lower(n):
    r = np.arange(n)
    return jnp.asarray(r[:, None] > r[None, :], dtype=BF16)


def _sb_prompt(proj, k, v, layer, bias, batch, seq):
    tq, tk = min(SB_TQ, seq), SB_TK
    nq = seq // tq
    nh = SB_HEADS_PER_STEP
    w = nh * DH_A
    return pl.pallas_call(
        functools.partial(_sb_prompt_kernel, tq=tq, tk=tk, nh=nh),
        grid_spec=pltpu.PrefetchScalarGridSpec(
            num_scalar_prefetch=1,
            grid=(batch, H_A // nh, nq),
            in_specs=[
                pl.BlockSpec((tq, w), lambda b, h, i, s: (b * nq + i, COL_QA // w + h)),
                pl.BlockSpec((None, seq, w), lambda b, h, i, s: (layer, b, h)),
                pl.BlockSpec((None, seq, w), lambda b, h, i, s: (layer, b, h)),
                pl.BlockSpec((tk, tk), lambda b, h, i, s: (0, 0)),
            ],
            out_specs=pl.BlockSpec((tq, w), lambda b, h, i, s: (b * nq + i, h)),
            scratch_shapes=[pltpu.VMEM((nh, tq, 1), F32)],
        ),
        out_shape=jax.ShapeDtypeStruct((batch * seq, D_MODEL), F32),
        compiler_params=_params("parallel", "parallel", "arbitrary"),
        name="sb_prompt",
    )(bias, proj, k, v, _strict_lower(tk))


def _sb_sample_kernel(pt_ref, q_ref, kn_ref, vn_ref, *rest, n_pages, page):
    kc_refs, vc_refs = rest[:n_pages], rest[n_pages:2 * n_pages]
    bias_ref, u_ref, o_ref, kpad, vpad = rest[2 * n_pages:]
    heads = [slice(h * DH_A, (h + 1) * DH_A) for h in range(H_A)]
    qs = [(q_ref[:, hs] * SB_QSCALE).astype(BF16) for hs in heads]
    rows = H_A * SUBLANE

    def block(k_of, v_of, width, before):
        z = jnp.concatenate([_dot_nt(q, k_of(h)) for h, q in enumerate(qs)], axis=0) + bias_ref[:, 0:width]
        sp = _sb_softplus2(z)
        drop = sp if before is None else jnp.where(before, sp, 0.0)
        w = _sb_weights(z, sp, drop, u_ref[0:width, 0:width], None)
        if before is not None:
            w = jnp.where(before, w, 0.0)
        pv = [_dot(w[h * SUBLANE:(h + 1) * SUBLANE].astype(BF16), v_of(h)) for h in range(H_A)]
        return pv, jnp.sum(drop, axis=1, keepdims=True)

    def cached(refs, ps):
        return lambda h: jnp.concatenate([refs[p][pl.ds(h, page, stride=H_A), :] for p in ps], axis=0).astype(BF16)

    kpad[...] = jnp.zeros_like(kpad)
    vpad[...] = jnp.zeros_like(vpad)
    kpad[0:SUBLANE, :] = kn_ref[...]
    vpad[0:SUBLANE, :] = vn_ref[...]
    t_of_row = lax.broadcasted_iota(jnp.int32, (rows, page), 0) & (SUBLANE - 1)
    before = lax.broadcasted_iota(jnp.int32, (rows, page), 1) < t_of_row
    blocks = [block(lambda h: kpad[:, heads[h]].astype(BF16), lambda h: vpad[:, heads[h]].astype(BF16),
                    page, before)]
    for hi_page in range(n_pages - 1, -1, -2):
        ps = [p for p in (hi_page - 1, hi_page) if p >= 0]
        blocks.append(block(cached(kc_refs, ps), cached(vc_refs, ps), len(ps) * page, None))
    run = jnp.zeros((rows, 1), F32)
    acc = [jnp.zeros((SUBLANE, DH_A), F32) for _ in range(H_A)]
    for pv, row_drop in blocks:
        scale = jnp.exp2(-run)
        acc = [acc[h] + scale[h * SUBLANE:(h + 1) * SUBLANE] * pv[h] for h in range(H_A)]
        run = run + row_drop
    o_ref[...] = jnp.concatenate(acc, axis=1)[0:o_ref.shape[0]]


def _sb_sample(layer, q, k_new, v_new, cache_k, cache_v, page_table, bias):
    nb, t_new, d = q.shape
    n_pages = page_table.shape[1]
    page = cache_k.shape[2] // H_A
    assert page == LANE and t_new <= SUBLANE
    pad_rows = ((0, 0), (0, SUBLANE - t_new), (0, 0))
    rows = H_A * SUBLANE
    wide = 2 * page
    bias_t = jnp.broadcast_to(jnp.repeat(bias * LOG2E, SUBLANE)[:, None], (rows, wide))
    row_spec = pl.BlockSpec((None, SUBLANE, d), lambda b, pt: (b, 0, 0))

    def page_spec(p):
        return pl.BlockSpec((None, None, page * H_A, DH_A), lambda b, pt: (layer, pt[b, p], 0, 0))

    pages = [page_spec(p) for p in range(n_pages)]
    return pl.pallas_call(
        functools.partial(_sb_sample_kernel, n_pages=n_pages, page=page),
        grid_spec=pltpu.PrefetchScalarGridSpec(
            num_scalar_prefetch=1,
            grid=(nb,),
            in_specs=[row_spec, row_spec, row_spec] + pages + pages + [
                pl.BlockSpec((rows, wide), lambda b, pt: (0, 0)),
                pl.BlockSpec((wide, wide), lambda b, pt: (0, 0)),
            ],
            out_specs=pl.BlockSpec((None, t_new, d), lambda b, pt: (b, 0, 0)),
            scratch_shapes=[pltpu.VMEM((page, d), F32), pltpu.VMEM((page, d), F32)],
        ),
        out_shape=jax.ShapeDtypeStruct((nb, t_new, d), F32),
        compiler_params=_params("parallel"),
        name="sb_sample",
    )(page_table, jnp.pad(q, pad_rows), jnp.pad(k_new, pad_rows), jnp.pad(v_new, pad_rows),
      *([cache_k] * n_pages), *([cache_v] * n_pages), bias_t, _strict_lower(wide))


def _gla_kernel(q_ref, k_ref, v_ref, al_ref, wg_ref, bg_ref, on_ref, s0_ref, tri_ref, s_all_ref, o_ref,
                sfin_ref, s_scr, *, chunk, t_valid):
    del s_all_ref
    c = pl.program_id(1)

    @pl.when(c == 0)
    def _():
        s_scr[...] = s0_ref[...]

    tri = tri_ref[...]
    lower = lax.broadcasted_iota(jnp.int32, (chunk, chunk), 0) >= lax.broadcasted_iota(jnp.int32, (chunk, chunk), 1)
    ones = jnp.ones((chunk, DK_B), BF16)
    for b in range(s_scr.shape[0]):
        gate = _dot(al_ref[b].astype(BF16), wg_ref[...]) + bg_ref[...]
        log_a = -_softplus(-gate) * (1.0 / GATE_NORM)
        if t_valid < chunk:
            rows = lax.broadcasted_iota(jnp.int32, log_a.shape, 0)
            log_a = jnp.where(rows < t_valid, log_a, 0.0)
        for h in range(H_B):
            ks = slice(h * DK_B, (h + 1) * DK_B)
            vs = slice(h * DV_B, (h + 1) * DV_B)
            hi, lo = _split_bf16(log_a[:, ks])
            cum = _dot(tri, hi) + _dot(tri, lo)
            last = cum[chunk - 1:chunk, :]
            last_col = jnp.exp(_dot_tn(hi, ones) + _dot_tn(lo, ones))
            q = q_ref[b, :, ks] * (DK_B ** -0.5)
            k = k_ref[b, :, ks]
            v = v_ref[b, :, vs].astype(BF16)
            q_dec = (q * jnp.exp(cum)).astype(BF16)
            mid = cum[chunk // 2:chunk // 2 + 1, :]
            q_mid = (q * jnp.exp(cum - mid)).astype(BF16)
            k_mid = (k * jnp.exp(mid - cum)).astype(BF16)
            att = jnp.where(lower, _dot_nt(q_mid, k_mid), 0.0)
            s = s_scr[b, h]
            o = _dot(q_dec, s.astype(BF16)) + _dot(att.astype(BF16), v)
            k_rem = (k * jnp.exp(last - cum)).astype(BF16)
            s_scr[b, h] = jnp.concatenate([last_col] * (DV_B // DK_B), axis=1) * s + _dot_tn(k_rem, v)
            o_ref[b, :, vs] = _rms(o) * on_ref[...]

    @pl.when(c == pl.num_programs(1) - 1)
    def _():
        sfin_ref[...] = s_scr[...]


GLA_SEQS_PER_STEP = 4


def _gla(proj, w_gk2, b_gk2, onorm_g, s0, s0_layer, s_all, layer, nb, seq, chunk, t_valid):
    nc = seq // chunk
    ns = GLA_SEQS_PER_STEP
    assert nb % ns == 0
    wg = jnp.zeros((LANE, H_B * DK_B), F32).at[:GATE_RANK].set(w_gk2).astype(BF16)
    r = np.arange(chunk)
    tri = jnp.asarray(r[:, None] >= r[None, :], dtype=BF16)
    kw = H_B * DK_B
    vw = H_B * DV_B
    p3 = proj.reshape(nb, seq, N_COLS)
    state_spec = pl.BlockSpec((None, ns, H_B, DK_B, DV_B), lambda b, c: (layer, b, 0, 0, 0))
    o, s_fin = pl.pallas_call(
        functools.partial(_gla_kernel, chunk=chunk, t_valid=t_valid),
        grid=(nb // ns, nc),
        in_specs=[
            pl.BlockSpec((ns, chunk, kw), lambda b, c: (b, c, COL_QB // kw)),
            pl.BlockSpec((ns, chunk, kw), lambda b, c: (b, c, COL_KB // kw)),
            pl.BlockSpec((ns, chunk, vw), lambda b, c: (b, c, COL_VB // vw)),
            pl.BlockSpec((ns, chunk, LANE), lambda b, c: (b, c, COL_AL // LANE)),
            pl.BlockSpec((LANE, kw), lambda b, c: (0, 0)),
            pl.BlockSpec((1, kw), lambda b, c: (0, 0)),
            pl.BlockSpec((1, DV_B), lambda b, c: (0, 0)),
            pl.BlockSpec((None, ns, H_B, DK_B, DV_B), lambda b, c: (s0_layer, b, 0, 0, 0)),
            pl.BlockSpec((chunk, chunk), lambda b, c: (0, 0)),
            pl.BlockSpec(memory_space=pl.ANY),
        ],
        out_specs=[
            pl.BlockSpec((ns, chunk, vw), lambda b, c: (b, c, 0)),
            state_spec,
        ],
        out_shape=[
            jax.ShapeDtypeStruct((nb, seq, vw), F32),
            jax.ShapeDtypeStruct(s_all.shape, F32),
        ],
        input_output_aliases={9: 1},
        scratch_shapes=[pltpu.VMEM((ns, H_B, DK_B, DV_B), F32)],
        compiler_params=_params("parallel", "arbitrary"),
        name="gla",
    )(p3, p3, p3, p3, wg, b_gk2.reshape(1, kw), onorm_g.reshape(1, DV_B), s0, tri, s_all)
    return o.reshape(nb * seq, vw), s_fin


def _mix_kernel(ga_ref, gb_ref, oa_ref, ob_ref, w_ref, x_ref, gt_ref, o_ref):
    def sig(t):
        return 1.0 / (1.0 + jnp.exp(-t))

    merged = sig(ga_ref[...]) * oa_ref[...] + sig(gb_ref[...]) * ob_ref[...]
    o_ref[...] = x_ref[...] + gt_ref[...] * _dot(merged.astype(BF16), w_ref[...])


def _mix_out(proj, o_a, o_b, w_out, x, gt, tm, rows_per_mod):
    n, d = x.shape
    r = gt.shape[1]
    row = lambda i: (i, 0)
    return pl.pallas_call(
        _mix_kernel,
        grid=(n // tm,),
        in_specs=[
            pl.BlockSpec((tm, d), lambda i: (i, COL_GA // d)),
            pl.BlockSpec((tm, d), lambda i: (i, COL_GB // d)),
            pl.BlockSpec((tm, d), row),
            pl.BlockSpec((tm, d), row),
            pl.BlockSpec((d, d), lambda i: (0, 0)),
            pl.BlockSpec((tm, d), row),
            pl.BlockSpec((None, r, d), lambda i: (i // rows_per_mod, 0, 0)),
        ],
        out_specs=pl.BlockSpec((tm, d), row),
        out_shape=jax.ShapeDtypeStruct((n, d), F32),
        compiler_params=_params("parallel"),
        name="mix_out",
    )(proj, proj, o_a, o_b, w_out, x, gt)


def _router_kernel(x_ref, g_ref, sc_ref, sh_ref, whi_ref, wlo_ref, b_ref, h_ref, wt_ref):
    h = _rms(x_ref[...]) * g_ref[...]
    h = h * (1.0 + sc_ref[...]) + sh_ref[...]
    h_ref[...] = h.astype(BF16)
    hi, lo = _split_bf16(h)
    lg = _dot(hi, whi_ref[...]) + (_dot(lo, whi_ref[...]) + _dot(hi, wlo_ref[...])) + b_ref[...]

    lane = lax.broadcasted_iota(jnp.int32, lg.shape, 1).astype(F32)
    past_end = float(ROUTER_PAD)

    def first_max(vals):
        m = jnp.max(vals, axis=1, keepdims=True)
        return m, jnp.min(jnp.where(vals == m, lane, past_end), axis=1, keepdims=True)

    is_group = lane < N_GROUPS
    g_max, g_idx = first_max(jnp.where(is_group, lg, -jnp.inf))
    g_w = 1.0 / jnp.sum(jnp.where(is_group, jnp.exp(lg - g_max), 0.0), axis=1, keepdims=True)
    first = N_GROUPS + g_idx * EXPERTS_PER_GROUP
    cand = jnp.where((lane >= first) & (lane < first + EXPERTS_PER_GROUP), lg, -jnp.inf)
    v1, i1 = first_max(cand)
    v2, i2 = first_max(jnp.where(lane == i1, -jnp.inf, cand))
    t = jnp.exp(v2 - v1)
    p1 = 1.0 / (1.0 + t)
    wt_ref[...] = jnp.where(lane == i1, p1 * g_w, 0.0) + jnp.where(lane == i2, t * p1 * g_w, 0.0)


def _router(x, g, sc, sh, w_r, b_r, tm, rows_per_mod):
    n, d = x.shape
    r = sc.shape[1]
    whi, wlo = _split_bf16(w_r)
    mod_spec = pl.BlockSpec((None, r, d), lambda i: (i // rows_per_mod, 0, 0))
    full = lambda i: (0, 0)
    return pl.pallas_call(
        _router_kernel,
        grid=(n // tm,),
        in_specs=[
            pl.BlockSpec((tm, d), lambda i: (i, 0)),
            pl.BlockSpec((1, d), full),
            mod_spec,
            mod_spec,
            pl.BlockSpec((d, ROUTER_PAD), full),
            pl.BlockSpec((d, ROUTER_PAD), full),
            pl.BlockSpec((1, ROUTER_PAD), full),
        ],
        out_specs=[
            pl.BlockSpec((tm, d), lambda i: (i, 0)),
            pl.BlockSpec((tm, ROUTER_PAD), lambda i: (i, 0)),
        ],
        out_shape=[
            jax.ShapeDtypeStruct((n, d), BF16),
            jax.ShapeDtypeStruct((n, ROUTER_PAD), F32),
        ],
        compiler_params=_params("parallel"),
        name="router",
    )(x, g, sc, sh, whi, wlo, b_r)


def _moe_kernel(h_ref, wt_ref, wg_ref, wu_ref, wd_ref, x_ref, gt_ref, o_ref, acc_scr):
    e = pl.program_id(1)

    @pl.when(e == 0)
    def _():
        acc_scr[...] = jnp.zeros_like(acc_scr)

    h = h_ref[...]
    lane = lax.broadcasted_iota(jnp.int32, wt_ref.shape, 1)
    we = jnp.sum(jnp.where(lane == e + N_GROUPS, wt_ref[...], 0.0), axis=1, keepdims=True)
    a = _dot(h, wg_ref[...])
    hid = a * (1.0 / (1.0 + jnp.exp(-a))) * _dot(h, wu_ref[...])
    hid = jnp.where(we != 0.0, hid * we, 0.0)
    acc_scr[...] += _dot(hid.astype(BF16), wd_ref[...])

    @pl.when(e == pl.num_programs(1) - 1)
    def _():
        o_ref[...] = x_ref[...] + gt_ref[...] * acc_scr[...]


def _moe(h, wt, layer, wg, wu, wd, x, gt, tm, rows_per_mod):
    n, d = x.shape
    r = gt.shape[1]
    row = lambda i, e: (i, 0)
    return pl.pallas_call(
        _moe_kernel,
        grid=(n // tm, N_EXPERTS),
        in_specs=[
            pl.BlockSpec((tm, d), row),
            pl.BlockSpec((tm, ROUTER_PAD), row),
            pl.BlockSpec((None, None, d, D_EXPERT), lambda i, e: (layer, e, 0, 0)),
            pl.BlockSpec((None, None, d, D_EXPERT), lambda i, e: (layer, e, 0, 0)),
            pl.BlockSpec((None, None, D_EXPERT, d), lambda i, e: (layer, e, 0, 0)),
            pl.BlockSpec((tm, d), row),
            pl.BlockSpec((None, r, d), lambda i, e: (i // rows_per_mod, 0, 0)),
        ],
        out_specs=pl.BlockSpec((tm, d), row),
        out_shape=jax.ShapeDtypeStruct((n, d), F32),
        scratch_shapes=[pltpu.VMEM((tm, d), F32)],
        compiler_params=_params("parallel", "arbitrary"),
        name="moe",
    )(h, wt, wg, wu, wd, x, gt)


def _layer_weights(l, w_in, qn_g, kn_g, w_out, w_rg, b_rg, w_re, b_re, experts):
    d = w_in.shape[1]
    offs = np.cumsum([0, H_A * DH_A, H_A * DH_A, H_A * DH_A, H_B * DK_B, H_B * DK_B, H_B * DV_B,
                      GATE_RANK, D_MODEL, D_MODEL])
    wl = w_in[l]
    seg = [wl[:, offs[i]:offs[i + 1]] for i in range(9)]
    qa, ka, va, qb, kb, vb, al, ga, gb = seg
    al = jnp.pad(al, ((0, 0), (0, AL_PAD - GATE_RANK)))
    w_proj = jnp.concatenate([qa, ka, va, qb, kb, vb, ga, gb, al], axis=1).astype(BF16)
    colgain = jnp.concatenate([jnp.tile(qn_g[l], H_A), jnp.tile(kn_g[l], H_A),
                               jnp.ones((N_COLS - COL_QB + D_MODEL,), F32)]).reshape(1, N_COLS + 2 * D_MODEL)
    pad = ROUTER_PAD - N_GROUPS - N_EXPERTS
    w_r = jnp.pad(jnp.concatenate([w_rg[l], w_re[l]], axis=1), ((0, 0), (0, pad)))
    b_r = jnp.pad(jnp.concatenate([b_rg[l], b_re[l]]), (0, pad)).reshape(1, ROUTER_PAD)
    return dict(w_proj=w_proj, colgain=colgain,
                w_out=w_out[l].astype(BF16), w_r=w_r, b_r=b_r, layer=l, experts=experts)


def _trunk_layer(x, carried, mod, lw, sb_fn, gla_fn, norm1_g, norm2_g, tm, rows_per_mod):
    sh1, sc1, gt1, sh2, sc2, gt2 = mod
    k_all, v_all, s_all = carried
    up = PROJ_ROW_SCALE if rows_per_mod % PROJ_ROW_SCALE == 0 else 1
    proj, k_all, v_all = _proj(x, norm1_g, sc1, sh1, lw["w_proj"], lw["colgain"], tm * up, rows_per_mod // up,
                               lw["layer"], k_all, v_all)
    o_a = sb_fn(proj, k_all, v_all)
    o_b, s_all = gla_fn(proj, s_all)
    tm_mix = min(tm, 512)
    x = _mix_out(proj, o_a, o_b, lw["w_out"], x, gt1, tm_mix, rows_per_mod * (tm // tm_mix))
    h, wt = _router(x, norm2_g, sc2, sh2, lw["w_r"], lw["b_r"], tm, rows_per_mod)
    x = _moe(h, wt, lw["layer"], *lw["experts"], x, gt2, tm, rows_per_mod)
    return x, (k_all, v_all, s_all)


def kernel(x_prompt, x_sample, cache_k, cache_v, state_gla, page_table, c_prompt, c_sample,
           norm1_g, norm2_g, w_ada, b_ada, w_in, qn_g, kn_g, sb_bias, w_gk2, b_gk2, onorm_g, w_out,
           w_rg, b_rg, w_re, b_re, w_e_gate, w_e_up, w_e_down):
    depth = w_in.shape[0]
    bp, seq, d = x_prompt.shape
    bs, t_s, _ = x_sample.shape
    n_pool, page = cache_k.shape[1], cache_k.shape[2]
    n_p, n_s = bp * seq, bs * t_s
    tm_p = min(1024, seq)
    assert seq % tm_p == 0 and n_s % SUBLANE == 0

    c_all = jnp.concatenate([c_prompt, c_sample], axis=0)
    r_pad = -c_all.shape[0] % SUBLANE
    mod_all = _ada(jnp.pad(c_all, ((0, r_pad), (0, 0))), w_ada, b_ada)

    ck = cache_k.reshape(depth, n_pool, page * H_A, DH_A)
    cv = cache_v.reshape(depth, n_pool, page * H_A, DH_A)
    gla_zero = jnp.zeros((1, bp, H_B, DK_B, DV_B), F32)
    experts = (w_e_gate.astype(BF16), w_e_up.astype(BF16), w_e_down.astype(BF16))

    yp = x_prompt.reshape(n_p, d)
    ys = x_sample.reshape(n_s, d)
    state_shape = (H_B, DK_B, DV_B)
    car_p = (jnp.zeros((depth, n_p, d), F32), jnp.zeros((depth, n_p, d), F32),
             jnp.zeros((depth, bp) + state_shape, F32))
    car_s = (jnp.zeros((depth, n_s, d), F32), jnp.zeros((depth, n_s, d), F32),
             jnp.zeros((depth, bs) + state_shape, F32))
    for l in range(depth):
        lw = _layer_weights(l, w_in, qn_g, kn_g, w_out, w_rg, b_rg, w_re, b_re, experts)
        n1, n2 = norm1_g[l].reshape(1, d), norm2_g[l].reshape(1, d)
        mod_p = [m.reshape(bp, 1, d) for m in jnp.split(mod_all[l, :bp], 6, axis=-1)]
        mod_s = [jnp.repeat(m, t_s, axis=0).reshape(1, n_s, d)
                 for m in jnp.split(mod_all[l, bp:bp + bs], 6, axis=-1)]

        def sb_p(proj, k_all, v_all, l=l):
            return _sb_prompt(proj, k_all, v_all, l, sb_bias[l], bp, seq)

        def gla_p(proj, s_all, l=l):
            return _gla(proj, w_gk2[l], b_gk2[l], onorm_g[l], gla_zero, 0, s_all, l,
                        nb=bp, seq=seq, chunk=min(GLA_CHUNK, seq), t_valid=min(GLA_CHUNK, seq))

        yp, car_p = _trunk_layer(yp, car_p, mod_p, lw, sb_p, gla_p, n1, n2, tm_p, seq // tm_p)

        def sb_s(proj, k_all, v_all, l=l):
            q = proj[:, COL_QA:COL_QB].reshape(bs, t_s, d)
            o = _sb_sample(l, q, k_all[l].reshape(bs, t_s, d), v_all[l].reshape(bs, t_s, d), ck, cv,
                           page_table, sb_bias[l])
            return o.reshape(n_s, d)

        def gla_s(proj, s_all, l=l):
            p3 = jnp.pad(proj.reshape(bs, t_s, N_COLS), ((0, 0), (0, SUBLANE - t_s), (0, 0)))
            o, s_all = _gla(p3.reshape(bs * SUBLANE, N_COLS), w_gk2[l], b_gk2[l], onorm_g[l], state_gla, l,
                            s_all, l, nb=bs, seq=SUBLANE, chunk=SUBLANE, t_valid=t_s)
            return o.reshape(bs, SUBLANE, d)[:, :t_s].reshape(n_s, d), s_all

        ys, car_s = _trunk_layer(ys, car_s, mod_s, lw, sb_s, gla_s, n1, n2, n_s, 1)

    kv_p, kv_s = (depth, bp, seq, H_A, DH_A), (depth, bs, t_s, H_A, DH_A)
    return (yp.reshape(bp, seq, d), ys.reshape(bs, t_s, d), car_p[0].reshape(kv_p), car_p[1].reshape(kv_p),
            car_p[2].astype(state_gla.dtype), car_s[0].reshape(kv_s), car_s[1].reshape(kv_s),
            car_s[2].astype(state_gla.dtype))
```

```python
import functools

import jax
import jax.numpy as jnp
import numpy as np
from jax import lax
from jax.experimental import pallas as pl
from jax.experimental.pallas import tpu as pltpu

F32 = jnp.float32
BF16 = jnp.bfloat16

LANE = 128
SUBLANE = 8
VMEM_LIMIT = 56 * 1024 * 1024

D_MODEL = 1024
DH_A = 128
H_A = D_MODEL // DH_A
H_B = 4
DK_B = D_MODEL // (2 * H_B)
DV_B = D_MODEL // H_B
GATE_RANK = 16
GATE_NORM = 16.0
GLA_CHUNK = 128
N_GROUPS = 4
EXPERTS_PER_GROUP = 8
N_EXPERTS = N_GROUPS * EXPERTS_PER_GROUP
TOP_K = 2
D_EXPERT = D_MODEL // 4
EPS = 1e-6

COL_QA = 0
COL_QB = COL_QA + H_A * DH_A
COL_KB = COL_QB + H_B * DK_B
COL_VB = COL_KB + H_B * DK_B
COL_GA = COL_VB + H_B * DV_B
COL_GB = COL_GA + D_MODEL
COL_AL = COL_GB + D_MODEL
AL_PAD = 512
N_COLS = COL_AL + AL_PAD
PROJ_TN = 512
PROJ_ROW_SCALE = 2
ROUTER_PAD = LANE


def _params(*sem):
    return pltpu.CompilerParams(dimension_semantics=sem, vmem_limit_bytes=VMEM_LIMIT)


def _softplus(z):
    return jnp.maximum(z, 0.0) + jnp.log1p(jnp.exp(-jnp.abs(z)))


def _split_bf16(x):
    hi = x.astype(BF16)
    lo = (x - hi.astype(F32)).astype(BF16)
    return hi, lo


def _dot(a, b):
    return jnp.dot(a, b, preferred_element_type=F32)


def _dot_nt(a, b):
    return lax.dot_general(a, b, (((1,), (1,)), ((), ())), preferred_element_type=F32)


def _dot_tn(a, b):
    return lax.dot_general(a, b, (((0,), (0,)), ((), ())), preferred_element_type=F32)


def _rms(x):
    return x * lax.rsqrt(jnp.mean(x * x, axis=-1, keepdims=True) + EPS)


def _ada_kernel(c_ref, w_ref, b_ref, o_ref):
    c = c_ref[...]
    a = (c * (1.0 / (1.0 + jnp.exp(-c)))).astype(BF16)
    o_ref[...] = _dot(a, w_ref[...].astype(BF16)) + b_ref[...]


def _ada(c, w_ada, b_ada):
    depth, d, n = w_ada.shape
    r = c.shape[0]
    tn = 1024
    return pl.pallas_call(
        _ada_kernel,
        grid=(depth, n // tn),
        in_specs=[
            pl.BlockSpec((r, d), lambda l, j: (0, 0)),
            pl.BlockSpec((None, d, tn), lambda l, j: (l, 0, j)),
            pl.BlockSpec((None, 1, tn), lambda l, j: (l, 0, j)),
        ],
        out_specs=pl.BlockSpec((None, r, tn), lambda l, j: (l, 0, j)),
        out_shape=jax.ShapeDtypeStruct((depth, r, n), F32),
        compiler_params=_params("parallel", "parallel"),
        name="ada",
    )(c, w_ada, b_ada.reshape(depth, 1, n))


def _modulate_rows(x_ref, g_ref, sc_ref, sh_ref, h_scr):
    @pl.when(pl.program_id(1) == 0)
    def _():
        h = _rms(x_ref[...]) * g_ref[...]
        h_scr[...] = (h * (1.0 + sc_ref[...]) + sh_ref[...]).astype(BF16)


def _project_tile(h_scr, w_ref, cg_ref, o_ref, head_norm):
    acc = _dot(h_scr[...], w_ref[...])
    if not head_norm:
        o_ref[...] = acc
        return
    for c in range(acc.shape[1] // DH_A):
        sl = slice(c * DH_A, (c + 1) * DH_A)
        o_ref[:, sl] = _rms(acc[:, sl]) * cg_ref[:, sl]


def _proj_kernel(x_ref, g_ref, sc_ref, sh_ref, w_ref, cg_ref, k_all_ref, v_all_ref, o_ref, k_ref, v_ref,
                 h_scr, *, nq):
    del k_all_ref, v_all_ref
    j = pl.program_id(1)
    _modulate_rows(x_ref, g_ref, sc_ref, sh_ref, h_scr)

    @pl.when(j < nq)
    def _():
        _project_tile(h_scr, w_ref, cg_ref, o_ref, True)

    @pl.when((j >= nq) & (j < 2 * nq))
    def _():
        _project_tile(h_scr, w_ref, cg_ref, k_ref, True)

    @pl.when((j >= 2 * nq) & (j < 3 * nq))
    def _():
        _project_tile(h_scr, w_ref, cg_ref, v_ref, False)

    @pl.when(j >= 3 * nq)
    def _():
        _project_tile(h_scr, w_ref, cg_ref, o_ref, False)


def _proj(x, g, sc, sh, w, colgain, tm, rows_per_mod, layer, k_all, v_all):
    n, d = x.shape
    r = sc.shape[1]
    tn = PROJ_TN
    nq = d // tn
    mod_spec = pl.BlockSpec((None, r, d), lambda i, j: (i // rows_per_mod, 0, 0))
    in_specs = [
        pl.BlockSpec((tm, d), lambda i, j: (i, 0)),
        pl.BlockSpec((1, d), lambda i, j: (0, 0)),
        mod_spec,
        mod_spec,
        pl.BlockSpec((d, tn), lambda i, j: (0, j)),
        pl.BlockSpec((1, tn), lambda i, j: (0, j)),
        pl.BlockSpec(memory_space=pl.ANY),
        pl.BlockSpec(memory_space=pl.ANY),
    ]
    kv_shape = jax.ShapeDtypeStruct(k_all.shape, F32)
    return pl.pallas_call(
        functools.partial(_proj_kernel, nq=nq),
        grid=(n // tm, w.shape[1] // tn),
        in_specs=in_specs,
        out_specs=[
            pl.BlockSpec((tm, tn), lambda i, j: (i, jnp.where(j < nq, j, jnp.maximum(j - 2 * nq, nq - 1)))),
            pl.BlockSpec((None, tm, tn), lambda i, j: (layer, i, jnp.clip(j - nq, 0, nq - 1))),
            pl.BlockSpec((None, tm, tn), lambda i, j: (layer, i, jnp.clip(j - 2 * nq, 0, nq - 1))),
        ],
        out_shape=[jax.ShapeDtypeStruct((n, N_COLS), F32), kv_shape, kv_shape],
        input_output_aliases={6: 1, 7: 2},
        scratch_shapes=[pltpu.VMEM((tm, d), BF16)],
        compiler_params=_params("parallel", "arbitrary"),
        name="proj",
    )(x, g, sc, sh, w, colgain, k_all, v_all)


SB_HEADS_PER_STEP = 4


SB_TQ = 512
SB_TK = 256
LOG2E = 1.4426950408889634
SB_QSCALE = DH_A ** -0.5 * LOG2E


def _sb_softplus2(z):
    neg_abs = lax.bitcast_convert_type(lax.bitcast_convert_type(z, jnp.uint32) | jnp.uint32(0x80000000), F32)
    return jnp.maximum(z, 0.0) + jnp.log(1.0 + jnp.exp2(neg_abs)) * LOG2E


def _sb_weights(z, sp, drop, u, run):
    t = z - sp - _dot(drop.astype(BF16), u)
    return jnp.exp2(t if run is None else t - run)


def _sb_prompt_kernel(bias_ref, q_ref, k_ref, v_ref, u_ref, o_ref, run_scr, *, tq, tk, nh):
    hg = pl.program_id(1)
    qi = pl.program_id(2)
    u = u_ref[...]
    heads = [slice(i * DH_A, (i + 1) * DH_A) for i in range(nh)]
    qs = [(q_ref[:, hs] * SB_QSCALE).astype(BF16) for hs in heads]
    bias = [bias_ref[hg * nh + i] * LOG2E for i in range(nh)]
    row = lax.broadcasted_iota(jnp.int32, (tq, tk), 0)
    col = lax.broadcasted_iota(jnp.int32, (tq, tk), 1)
    o_ref[...] = jnp.zeros_like(o_ref)
    run_scr[...] = jnp.zeros_like(run_scr)

    def block(off, visible, r0=0):
        for i, hs in enumerate(heads):
            k = k_ref[pl.ds(off, tk), hs].astype(BF16)
            v = v_ref[pl.ds(off, tk), hs].astype(BF16)
            z = _dot_nt(qs[i][r0:], k) + bias[i]
            sp = _sb_softplus2(z)
            drop = sp if visible is None else jnp.where(visible[r0:], sp, 0.0)
            w = _sb_weights(z, sp, drop, u, run_scr[i, r0:])
            if visible is not None:
                w = jnp.where(visible[r0:], w, 0.0)
            o_ref[r0:, hs] += _dot(w.astype(BF16), v)
            run_scr[i, r0:] += jnp.sum(drop, axis=1, keepdims=True)

    for d in reversed(range(tq // tk)):
        block(pl.multiple_of(qi * tq + d * tk, tk), col + d * tk < row, d * tk)

    def body(jj, carry):
        block(pl.multiple_of(qi * tq - (jj + 1) * tk, tk), None)
        return carry

    lax.fori_loop(0, qi * (tq // tk), body, 0)


def _strict_lower(n):
    r = np.arange(n)
    return jnp.asarray(r[:, None] > r[None, :], dtype=BF16)


def _sb_prompt(proj, k, v, layer, bias, batch, seq):
    tq, tk = min(SB_TQ, seq), SB_TK
    nq = seq // tq
    nh = SB_HEADS_PER_STEP
    w = nh * DH_A
    return pl.pallas_call(
        functools.partial(_sb_prompt_kernel, tq=tq, tk=tk, nh=nh),
        grid_spec=pltpu.PrefetchScalarGridSpec(
            num_scalar_prefetch=1,
            grid=(batch, H_A // nh, nq),
            in_specs=[
                pl.BlockSpec((tq, w), lambda b, h, i, s: (b * nq + i, COL_QA // w + h)),
                pl.BlockSpec((None, seq, w), lambda b, h, i, s: (layer, b, h)),
                pl.BlockSpec((None, seq, w), lambda b, h, i, s: (layer, b, h)),
                pl.BlockSpec((tk, tk), lambda b, h, i, s: (0, 0)),
            ],
            out_specs=pl.BlockSpec((tq, w), lambda b, h, i, s: (b * nq + i, h)),
            scratch_shapes=[pltpu.VMEM((nh, tq, 1), F32)],
        ),
        out_shape=jax.ShapeDtypeStruct((batch * seq, D_MODEL), F32),
        compiler_params=_params("parallel", "parallel", "arbitrary"),
        name="sb_prompt",
    )(bias, proj, k, v, _strict_lower(tk))


def _sb_sample_kernel(pt_ref, q_ref, kn_ref, vn_ref, *rest, n_pages, page):
    kc_refs, vc_refs = rest[:n_pages], rest[n_pages:2 * n_pages]
    bias_ref, u_ref, o_ref, kpad, vpad = rest[2 * n_pages:]
    heads = [slice(h * DH_A, (h + 1) * DH_A) for h in range(H_A)]
    qs = [(q_ref[:, hs] * SB_QSCALE).astype(BF16) for hs in heads]
    rows = H_A * SUBLANE

    def block(k_of, v_of, width, before):
        z = jnp.concatenate([_dot_nt(q, k_of(h)) for h, q in enumerate(qs)], axis=0) + bias_ref[:, 0:width]
        sp = _sb_softplus2(z)
        drop = sp if before is None else jnp.where(before, sp, 0.0)
        w = _sb_weights(z, sp, drop, u_ref[0:width, 0:width], None)
        if before is not None:
            w = jnp.where(before, w, 0.0)
        pv = [_dot(w[h * SUBLANE:(h + 1) * SUBLANE].astype(BF16), v_of(h)) for h in range(H_A)]
        return pv, jnp.sum(drop, axis=1, keepdims=True)

    def cached(refs, ps):
        return lambda h: jnp.concatenate([refs[p][pl.ds(h, page, stride=H_A), :] for p in ps], axis=0).astype(BF16)

    kpad[...] = jnp.zeros_like(kpad)
    vpad[...] = jnp.zeros_like(vpad)
    kpad[0:SUBLANE, :] = kn_ref[...]
    vpad[0:SUBLANE, :] = vn_ref[...]
    t_of_row = lax.broadcasted_iota(jnp.int32, (rows, page), 0) & (SUBLANE - 1)
    before = lax.broadcasted_iota(jnp.int32, (rows, page), 1) < t_of_row
    blocks = [block(lambda h: kpad[:, heads[h]].astype(BF16), lambda h: vpad[:, heads[h]].astype(BF16),
                    page, before)]
    for hi_page in range(n_pages - 1, -1, -2):
        ps = [p for p in (hi_page - 1, hi_page) if p >= 0]
        blocks.append(block(cached(kc_refs, ps), cached(vc_refs, ps), len(ps) * page, None))
    run = jnp.zeros((rows, 1), F32)
    acc = [jnp.zeros((SUBLANE, DH_A), F32) for _ in range(H_A)]
    for pv, row_drop in blocks:
        scale = jnp.exp2(-run)
        acc = [acc[h] + scale[h * SUBLANE:(h + 1) * SUBLANE] * pv[h] for h in range(H_A)]
        run = run + row_drop
    o_ref[...] = jnp.concatenate(acc, axis=1)[0:o_ref.shape[0]]


def _sb_sample(layer, q, k_new, v_new, cache_k, cache_v, page_table, bias):
    nb, t_new, d = q.shape
    n_pages = page_table.shape[1]
    page = cache_k.shape[2] // H_A
    assert page == LANE and t_new <= SUBLANE
    pad_rows = ((0, 0), (0, SUBLANE - t_new), (0, 0))
    rows = H_A * SUBLANE
    wide = 2 * page
    bias_t = jnp.broadcast_to(jnp.repeat(bias * LOG2E, SUBLANE)[:, None], (rows, wide))
    row_spec = pl.BlockSpec((None, SUBLANE, d), lambda b, pt: (b, 0, 0))

    def page_spec(p):
        return pl.BlockSpec((None, None, page * H_A, DH_A), lambda b, pt: (layer, pt[b, p], 0, 0))

    pages = [page_spec(p) for p in range(n_pages)]
    return pl.pallas_call(
        functools.partial(_sb_sample_kernel, n_pages=n_pages, page=page),
        grid_spec=pltpu.PrefetchScalarGridSpec(
            num_scalar_prefetch=1,
            grid=(nb,),
            in_specs=[row_spec, row_spec, row_spec] + pages + pages + [
                pl.BlockSpec((rows, wide), lambda b, pt: (0, 0)),
                pl.BlockSpec((wide, wide), lambda b, pt: (0, 0)),
            ],
            out_specs=pl.BlockSpec((None, t_new, d), lambda b, pt: (b, 0, 0)),
            scratch_shapes=[pltpu.VMEM((page, d), F32), pltpu.VMEM((page, d), F32)],
        ),
        out_shape=jax.ShapeDtypeStruct((nb, t_new, d), F32),
        compiler_params=_params("parallel"),
        name="sb_sample",
    )(page_table, jnp.pad(q, pad_rows), jnp.pad(k_new, pad_rows), jnp.pad(v_new, pad_rows),
      *([cache_k] * n_pages), *([cache_v] * n_pages), bias_t, _strict_lower(wide))


def _gla_kernel(q_ref, k_ref, v_ref, al_ref, wg_ref, bg_ref, on_ref, s0_ref, tri_ref, s_all_ref, o_ref,
                sfin_ref, s_scr, *, chunk, t_valid):
    del s_all_ref
    c = pl.program_id(1)

    @pl.when(c == 0)
    def _():
        s_scr[...] = s0_ref[...]

    tri = tri_ref[...]
    lower = lax.broadcasted_iota(jnp.int32, (chunk, chunk), 0) >= lax.broadcasted_iota(jnp.int32, (chunk, chunk), 1)
    ones = jnp.ones((chunk, DK_B), BF16)
    for b in range(s_scr.shape[0]):
        gate = _dot(al_ref[b].astype(BF16), wg_ref[...]) + bg_ref[...]
        log_a = -_softplus(-gate) * (1.0 / GATE_NORM)
        if t_valid < chunk:
            rows = lax.broadcasted_iota(jnp.int32, log_a.shape, 0)
            log_a = jnp.where(rows < t_valid, log_a, 0.0)
        for h in range(H_B):
            ks = slice(h * DK_B, (h + 1) * DK_B)
            vs = slice(h * DV_B, (h + 1) * DV_B)
            hi, lo = _split_bf16(log_a[:, ks])
            cum = _dot(tri, hi) + _dot(tri, lo)
            last = cum[chunk - 1:chunk, :]
            last_col = jnp.exp(_dot_tn(hi, ones) + _dot_tn(lo, ones))
            q = q_ref[b, :, ks] * (DK_B ** -0.5)
            k = k_ref[b, :, ks]
            v = v_ref[b, :, vs].astype(BF16)
            q_dec = (q * jnp.exp(cum)).astype(BF16)
            mid = cum[chunk // 2:chunk // 2 + 1, :]
            q_mid = (q * jnp.exp(cum - mid)).astype(BF16)
            k_mid = (k * jnp.exp(mid - cum)).astype(BF16)
            att = jnp.where(lower, _dot_nt(q_mid, k_mid), 0.0)
            s = s_scr[b, h]
            o = _dot(q_dec, s.astype(BF16)) + _dot(att.astype(BF16), v)
            k_rem = (k * jnp.exp(last - cum)).astype(BF16)
            s_scr[b, h] = jnp.concatenate([last_col] * (DV_B // DK_B), axis=1) * s + _dot_tn(k_rem, v)
            o_ref[b, :, vs] = _rms(o) * on_ref[...]

    @pl.when(c == pl.num_programs(1) - 1)
    def _():
        sfin_ref[...] = s_scr[...]


GLA_SEQS_PER_STEP = 4


def _gla(proj, w_gk2, b_gk2, onorm_g, s0, s0_layer, s_all, layer, nb, seq, chunk, t_valid):
    nc = seq // chunk
    ns = GLA_SEQS_PER_STEP
    assert nb % ns == 0
    wg = jnp.zeros((LANE, H_B * DK_B), F32).at[:GATE_RANK].set(w_gk2).astype(BF16)
    r = np.arange(chunk)
    tri = jnp.asarray(r[:, None] >= r[None, :], dtype=BF16)
    kw = H_B * DK_B
    vw = H_B * DV_B
    p3 = proj.reshape(nb, seq, N_COLS)
    state_spec = pl.BlockSpec((None, ns, H_B, DK_B, DV_B), lambda b, c: (layer, b, 0, 0, 0))
    o, s_fin = pl.pallas_call(
        functools.partial(_gla_kernel, chunk=chunk, t_valid=t_valid),
        grid=(nb // ns, nc),
        in_specs=[
            pl.BlockSpec((ns, chunk, kw), lambda b, c: (b, c, COL_QB // kw)),
            pl.BlockSpec((ns, chunk, kw), lambda b, c: (b, c, COL_KB // kw)),
            pl.BlockSpec((ns, chunk, vw), lambda b, c: (b, c, COL_VB // vw)),
            pl.BlockSpec((ns, chunk, LANE), lambda b, c: (b, c, COL_AL // LANE)),
            pl.BlockSpec((LANE, kw), lambda b, c: (0, 0)),
            pl.BlockSpec((1, kw), lambda b, c: (0, 0)),
            pl.BlockSpec((1, DV_B), lambda b, c: (0, 0)),
            pl.BlockSpec((None, ns, H_B, DK_B, DV_B), lambda b, c: (s0_layer, b, 0, 0, 0)),
            pl.BlockSpec((chunk, chunk), lambda b, c: (0, 0)),
            pl.BlockSpec(memory_space=pl.ANY),
        ],
        out_specs=[
            pl.BlockSpec((ns, chunk, vw), lambda b, c: (b, c, 0)),
            state_spec,
        ],
        out_shape=[
            jax.ShapeDtypeStruct((nb, seq, vw), F32),
            jax.ShapeDtypeStruct(s_all.shape, F32),
        ],
        input_output_aliases={9: 1},
        scratch_shapes=[pltpu.VMEM((ns, H_B, DK_B, DV_B), F32)],
        compiler_params=_params("parallel", "arbitrary"),
        name="gla",
    )(p3, p3, p3, p3, wg, b_gk2.reshape(1, kw), onorm_g.reshape(1, DV_B), s0, tri, s_all)
    return o.reshape(nb * seq, vw), s_fin


def _mix_kernel(ga_ref, gb_ref, oa_ref, ob_ref, w_ref, x_ref, gt_ref, o_ref):
    def sig(t):
        return 1.0 / (1.0 + jnp.exp(-t))

    merged = sig(ga_ref[...]) * oa_ref[...] + sig(gb_ref[...]) * ob_ref[...]
    o_ref[...] = x_ref[...] + gt_ref[...] * _dot(merged.astype(BF16), w_ref[...])


def _mix_out(proj, o_a, o_b, w_out, x, gt, tm, rows_per_mod):
    n, d = x.shape
    r = gt.shape[1]
    row = lambda i: (i, 0)
    return pl.pallas_call(
        _mix_kernel,
        grid=(n // tm,),
        in_specs=[
            pl.BlockSpec((tm, d), lambda i: (i, COL_GA // d)),
            pl.BlockSpec((tm, d), lambda i: (i, COL_GB // d)),
            pl.BlockSpec((tm, d), row),
            pl.BlockSpec((tm, d), row),
            pl.BlockSpec((d, d), lambda i: (0, 0)),
            pl.BlockSpec((tm, d), row),
            pl.BlockSpec((None, r, d), lambda i: (i // rows_per_mod, 0, 0)),
        ],
        out_specs=pl.BlockSpec((tm, d), row),
        out_shape=jax.ShapeDtypeStruct((n, d), F32),
        compiler_params=_params("parallel"),
        name="mix_out",
    )(proj, proj, o_a, o_b, w_out, x, gt)


def _router_kernel(x_ref, g_ref, sc_ref, sh_ref, whi_ref, wlo_ref, b_ref, h_ref, wt_ref):
    h = _rms(x_ref[...]) * g_ref[...]
    h = h * (1.0 + sc_ref[...]) + sh_ref[...]
    h_ref[...] = h.astype(BF16)
    hi, lo = _split_bf16(h)
    lg = _dot(hi, whi_ref[...]) + (_dot(lo, whi_ref[...]) + _dot(hi, wlo_ref[...])) + b_ref[...]

    lane = lax.broadcasted_iota(jnp.int32, lg.shape, 1).astype(F32)
    past_end = float(ROUTER_PAD)

    def first_max(vals):
        m = jnp.max(vals, axis=1, keepdims=True)
        return m, jnp.min(jnp.where(vals == m, lane, past_end), axis=1, keepdims=True)

    is_group = lane < N_GROUPS
    g_max, g_idx = first_max(jnp.where(is_group, lg, -jnp.inf))
    g_w = 1.0 / jnp.sum(jnp.where(is_group, jnp.exp(lg - g_max), 0.0), axis=1, keepdims=True)
    first = N_GROUPS + g_idx * EXPERTS_PER_GROUP
    cand = jnp.where((lane >= first) & (lane < first + EXPERTS_PER_GROUP), lg, -jnp.inf)
    v1, i1 = first_max(cand)
    v2, i2 = first_max(jnp.where(lane == i1, -jnp.inf, cand))
    t = jnp.exp(v2 - v1)
    p1 = 1.0 / (1.0 + t)
    wt_ref[...] = jnp.where(lane == i1, p1 * g_w, 0.0) + jnp.where(lane == i2, t * p1 * g_w, 0.0)


def _router(x, g, sc, sh, w_r, b_r, tm, rows_per_mod):
    n, d = x.shape
    r = sc.shape[1]
    whi, wlo = _split_bf16(w_r)
    mod_spec = pl.BlockSpec((None, r, d), lambda i: (i // rows_per_mod, 0, 0))
    full = lambda i: (0, 0)
    return pl.pallas_call(
        _router_kernel,
        grid=(n // tm,),
        in_specs=[
            pl.BlockSpec((tm, d), lambda i: (i, 0)),
            pl.BlockSpec((1, d), full),
            mod_spec,
            mod_spec,
            pl.BlockSpec((d, ROUTER_PAD), full),
            pl.BlockSpec((d, ROUTER_PAD), full),
            pl.BlockSpec((1, ROUTER_PAD), full),
        ],
        out_specs=[
            pl.BlockSpec((tm, d), lambda i: (i, 0)),
            pl.BlockSpec((tm, ROUTER_PAD), lambda i: (i, 0)),
        ],
        out_shape=[
            jax.ShapeDtypeStruct((n, d), BF16),
            jax.ShapeDtypeStruct((n, ROUTER_PAD), F32),
        ],
        compiler_params=_params("parallel"),
        name="router",
    )(x, g, sc, sh, whi, wlo, b_r)


def _moe_kernel(h_ref, wt_ref, wg_ref, wu_ref, wd_ref, x_ref, gt_ref, o_ref, acc_scr):
    e = pl.program_id(1)

    @pl.when(e == 0)
    def _():
        acc_scr[...] = jnp.zeros_like(acc_scr)

    h = h_ref[...]
    lane = lax.broadcasted_iota(jnp.int32, wt_ref.shape, 1)
    we = jnp.sum(jnp.where(lane == e + N_GROUPS, wt_ref[...], 0.0), axis=1, keepdims=True)
    a = _dot(h, wg_ref[...].astype(BF16))
    hid = a * (1.0 / (1.0 + jnp.exp(-a))) * _dot(h, wu_ref[...].astype(BF16))
    hid = jnp.where(we != 0.0, hid * we, 0.0)
    acc_scr[...] += _dot(hid.astype(BF16), wd_ref[...].astype(BF16))

    @pl.when(e == pl.num_programs(1) - 1)
    def _():
        o_ref[...] = x_ref[...] + gt_ref[...] * acc_scr[...]


def _moe(h, wt, layer, wg, wu, wd, x, gt, tm, rows_per_mod):
    n, d = x.shape
    r = gt.shape[1]
    row = lambda i, e: (i, 0)
    return pl.pallas_call(
        _moe_kernel,
        grid=(n // tm, N_EXPERTS),
        in_specs=[
            pl.BlockSpec((tm, d), row),
            pl.BlockSpec((tm, ROUTER_PAD), row),
            pl.BlockSpec((None, None, d, D_EXPERT), lambda i, e: (layer, e, 0, 0)),
            pl.BlockSpec((None, None, d, D_EXPERT), lambda i, e: (layer, e, 0, 0)),
            pl.BlockSpec((None, None, D_EXPERT, d), lambda i, e: (layer, e, 0, 0)),
            pl.BlockSpec((tm, d), row),
            pl.BlockSpec((None, r, d), lambda i, e: (i // rows_per_mod, 0, 0)),
        ],
        out_specs=pl.BlockSpec((tm, d), row),
        out_shape=jax.ShapeDtypeStruct((n, d), F32),
        scratch_shapes=[pltpu.VMEM((tm, d), F32)],
        compiler_params=_params("parallel", "arbitrary"),
        name="moe",
    )(h, wt, wg, wu, wd, x, gt)


def _layer_weights(l, w_in, qn_g, kn_g, w_out, w_rg, b_rg, w_re, b_re, experts):
    d = w_in.shape[1]
    offs = np.cumsum([0, H_A * DH_A, H_A * DH_A, H_A * DH_A, H_B * DK_B, H_B * DK_B, H_B * DV_B,
                      GATE_RANK, D_MODEL, D_MODEL])
    wl = w_in[l]
    seg = [wl[:, offs[i]:offs[i + 1]] for i in range(9)]
    qa, ka, va, qb, kb, vb, al, ga, gb = seg
    al = jnp.pad(al, ((0, 0), (0, AL_PAD - GATE_RANK)))
    w_proj = jnp.concatenate([qa, ka, va, qb, kb, vb, ga, gb, al], axis=1).astype(BF16)
    colgain = jnp.concatenate([jnp.tile(qn_g[l], H_A), jnp.tile(kn_g[l], H_A),
                               jnp.ones((N_COLS - COL_QB + D_MODEL,), F32)]).reshape(1, N_COLS + 2 * D_MODEL)
    pad = ROUTER_PAD - N_GROUPS - N_EXPERTS
    w_r = jnp.pad(jnp.concatenate([w_rg[l], w_re[l]], axis=1), ((0, 0), (0, pad)))
    b_r = jnp.pad(jnp.concatenate([b_rg[l], b_re[l]]), (0, pad)).reshape(1, ROUTER_PAD)
    return dict(w_proj=w_proj, colgain=colgain,
                w_out=w_out[l].astype(BF16), w_r=w_r, b_r=b_r, layer=l, experts=experts)


def _trunk_layer(x, carried, mod, lw, sb_fn, gla_fn, norm1_g, norm2_g, tm, rows_per_mod):
    sh1, sc1, gt1, sh2, sc2, gt2 = mod
    k_all, v_all, s_all = carried
    up = PROJ_ROW_SCALE if rows_per_mod % PROJ_ROW_SCALE == 0 else 1
    proj, k_all, v_all = _proj(x, norm1_g, sc1, sh1, lw["w_proj"], lw["colgain"], tm * up, rows_per_mod // up,
                               lw["layer"], k_all, v_all)
    o_a = sb_fn(proj, k_all, v_all)
    o_b, s_all = gla_fn(proj, s_all)
    tm_mix = min(tm, 512)
    x = _mix_out(proj, o_a, o_b, lw["w_out"], x, gt1, tm_mix, rows_per_mod * (tm // tm_mix))
    h, wt = _router(x, norm2_g, sc2, sh2, lw["w_r"], lw["b_r"], tm, rows_per_mod)
    x = _moe(h, wt, lw["layer"], *lw["experts"], x, gt2, tm, rows_per_mod)
    return x, (k_all, v_all, s_all)


def kernel(x_prompt, x_sample, cache_k, cache_v, state_gla, page_table, c_prompt, c_sample,
           norm1_g, norm2_g, w_ada, b_ada, w_in, qn_g, kn_g, sb_bias, w_gk2, b_gk2, onorm_g, w_out,
           w_rg, b_rg, w_re, b_re, w_e_gate, w_e_up, w_e_down):
    depth = w_in.shape[0]
    bp, seq, d = x_prompt.shape
    bs, t_s, _ = x_sample.shape
    n_pool, page = cache_k.shape[1], cache_k.shape[2]
    n_p, n_s = bp * seq, bs * t_s
    tm_p = min(1024, seq)
    assert seq % tm_p == 0 and n_s % SUBLANE == 0

    c_all = jnp.concatenate([c_prompt, c_sample], axis=0)
    r_pad = -c_all.shape[0] % SUBLANE
    mod_all = _ada(jnp.pad(c_all, ((0, r_pad), (0, 0))), w_ada, b_ada)

    ck = cache_k.reshape(depth, n_pool, page * H_A, DH_A)
    cv = cache_v.reshape(depth, n_pool, page * H_A, DH_A)
    gla_zero = jnp.zeros((1, bp, H_B, DK_B, DV_B), F32)
    experts = (w_e_gate, w_e_up, w_e_down)

    yp = x_prompt.reshape(n_p, d)
    ys = x_sample.reshape(n_s, d)
    state_shape = (H_B, DK_B, DV_B)
    car_p = (jnp.zeros((depth, n_p, d), F32), jnp.zeros((depth, n_p, d), F32),
             jnp.zeros((depth, bp) + state_shape, F32))
    car_s = (jnp.zeros((depth, n_s, d), F32), jnp.zeros((depth, n_s, d), F32),
             jnp.zeros((depth, bs) + state_shape, F32))
    for l in range(depth):
        lw = _layer_weights(l, w_in, qn_g, kn_g, w_out, w_rg, b_rg, w_re, b_re, experts)
        n1, n2 = norm1_g[l].reshape(1, d), norm2_g[l].reshape(1, d)
        mod_p = [m.reshape(bp, 1, d) for m in jnp.split(mod_all[l, :bp], 6, axis=-1)]
        mod_s = [jnp.repeat(m, t_s, axis=0).reshape(1, n_s, d)
                 for m in jnp.split(mod_all[l, bp:bp + bs], 6, axis=-1)]

        def sb_p(proj, k_all, v_all, l=l):
            return _sb_prompt(proj, k_all, v_all, l, sb_bias[l], bp, seq)

        def gla_p(proj, s_all, l=l):
            return _gla(proj, w_gk2[l], b_gk2[l], onorm_g[l], gla_zero, 0, s_all, l,
                        nb=bp, seq=seq, chunk=min(GLA_CHUNK, seq), t_valid=min(GLA_CHUNK, seq))

        yp, car_p = _trunk_layer(yp, car_p, mod_p, lw, sb_p, gla_p, n1, n2, tm_p, seq // tm_p)

        def sb_s(proj, k_all, v_all, l=l):
            q = proj[:, COL_QA:COL_QB].reshape(bs, t_s, d)
            o = _sb_sample(l, q, k_all[l].reshape(bs, t_s, d), v_all[l].reshape(bs, t_s, d), ck, cv,
                           page_table, sb_bias[l])
            return o.reshape(n_s, d)

        def gla_s(proj, s_all, l=l):
            p3 = jnp.pad(proj.reshape(bs, t_s, N_COLS), ((0, 0), (0, SUBLANE - t_s), (0, 0)))
            o, s_all = _gla(p3.reshape(bs * SUBLANE, N_COLS), w_gk2[l], b_gk2[l], onorm_g[l], state_gla, l,
                            s_all, l, nb=bs, seq=SUBLANE, chunk=SUBLANE, t_valid=t_s)
            return o.reshape(bs, SUBLANE, d)[:, :t_s].reshape(n_s, d), s_all

        ys, car_s = _trunk_layer(ys, car_s, mod_s, lw, sb_s, gla_s, n1, n2, n_s, 1)

    kv_p, kv_s = (depth, bp, seq, H_A, DH_A), (depth, bs, t_s, H_A, DH_A)
    return (yp.reshape(bp, seq, d), ys.reshape(bs, t_s, d), car_p[0].reshape(kv_p), car_p[1].reshape(kv_p),
            car_p[2].astype(state_gla.dtype), car_s[0].reshape(kv_s), car_s[1].reshape(kv_s),
            car_s[2].astype(state_gla.dtype))
```
